```python
import jax, jax.numpy as jnp
from jax import lax
import numpy as np

D_MODEL = 1024
BATCH = 8
SEQ = 8192
DEPTH = 1
DEC_BATCH = 8
DEC_SEQ = 64
PAST_LEN = 1024

CHUNK = 64
D_MIX = D_MODEL
D_ATTN = D_MIX // 2
D_CONV = D_MIX - D_ATTN
HEAD_DIM = 64
N_HEADS = D_ATTN // HEAD_DIM
CONV_WIDTH = 31
CONV_STATE = CONV_WIDTH - 1
Q_BLOCK = 128
PEER_HEADS = 8
PEER_NKEYS = 128
PEER_EXPERTS = PEER_NKEYS * PEER_NKEYS
PEER_QDIM = 256
PEER_HALF = PEER_QDIM // 2
PEER_TOPK = 16
PEER_TOK_BLOCK = 128
D_IN_PROJ = 3 * D_ATTN + N_HEADS + 2 * D_CONV
RMS_EPS = 1e-6
LN_EPS = 1e-5

kernel_name = 'hymba_conformer_fox_peer_stream_step'


def rms_norm(x, g):
    xf = x.astype(jnp.float32)
    y = xf * lax.rsqrt(jnp.mean(xf * xf, axis=-1, keepdims=True) + RMS_EPS)
    return (y * g.astype(jnp.float32)).astype(x.dtype)


def layer_norm(x, g, b):
    xf = x.astype(jnp.float32)
    mu = jnp.mean(xf, axis=-1, keepdims=True)
    xc = xf - mu
    var = jnp.mean(xc * xc, axis=-1, keepdims=True)
    y = xc * lax.rsqrt(var + LN_EPS) * g.astype(jnp.float32) + b.astype(jnp.float32)
    return y.astype(x.dtype)


def fox_attention(q, k, v, logf, k_past, v_past, logf_past):
    B, T, H, Dh = q.shape
    P = k_past.shape[1]
    L = P + T
    k_all = jnp.concatenate([k_past, k], axis=1).astype(jnp.float32)
    v_all = jnp.concatenate([v_past, v], axis=1).astype(jnp.float32)
    c_all = jnp.cumsum(jnp.concatenate([logf_past, logf], axis=1).astype(jnp.float32), axis=1)
    c_k = jnp.transpose(c_all, (0, 2, 1))
    key_pos = jnp.arange(L)
    qb = Q_BLOCK if T % Q_BLOCK == 0 else T
    nb = T // qb
    scale = HEAD_DIM ** -0.5
    q_blocks = jnp.transpose((q.astype(jnp.float32) * scale).reshape(B, nb, qb, H, Dh), (1, 0, 2, 3, 4))
    cq_blocks = jnp.transpose(c_all[:, P:].reshape(B, nb, qb, H), (1, 0, 3, 2))
    qpos_blocks = (P + jnp.arange(T)).reshape(nb, qb)

    def one_block(args):
        q_blk, cq, qpos = args
        s = jnp.einsum('bqhd,bkhd->bhqk', q_blk, k_all)
        s = s + cq[..., :, None] - c_k[:, :, None, :]
        s = jnp.where(key_pos[None, None, None, :] <= qpos[None, None, :, None], s, -jnp.inf)
        p = jax.nn.softmax(s, axis=-1)
        return jnp.einsum('bhqk,bkhd->bqhd', p, v_all)

    o = lax.map(one_block, (q_blocks, cq_blocks, qpos_blocks))
    return jnp.transpose(o, (1, 0, 2, 3, 4)).reshape(B, T, H * Dh).astype(q.dtype)


def conv_module(a, gate, conv_past, w_dw, b_dw, ln_g, ln_b):
    u = a * jax.nn.sigmoid(gate)
    u_full = jnp.concatenate([conv_past.astype(u.dtype), u], axis=1)
    y = lax.conv_general_dilated(u_full, w_dw[:, None, :].astype(u.dtype), window_strides=(1,),
                                 padding='VALID', dimension_numbers=('NWC', 'WIO', 'NWC'),
                                 feature_group_count=u.shape[-1])
    y = layer_norm(y + b_dw, ln_g, ln_b)
    return jax.nn.silu(y), u_full[:, -CONV_STATE:]


def peer_ffn(x, w_pq, sub_keys, u_emb, v_emb):
    B, T, D = x.shape
    n = B * T
    pad = (-n) % PEER_TOK_BLOCK
    xt = jnp.pad(x.reshape(n, D), ((0, pad), (0, 0)))
    nb = xt.shape[0] // PEER_TOK_BLOCK
    keys = sub_keys.astype(jnp.float32)

    def block(xb):
        t = xb.shape[0]
        q = (xb @ w_pq).astype(jnp.float32).reshape(t, PEER_HEADS, 2, PEER_HALF)
        s = jnp.einsum('thpd,phkd->thpk', q, keys)
        top_s, top_i = lax.top_k(s, PEER_TOPK)
        cand_s = (top_s[:, :, 0, :, None] + top_s[:, :, 1, None, :]).reshape(t, PEER_HEADS, PEER_TOPK * PEER_TOPK)
        cand_i = (top_i[:, :, 0, :, None] * PEER_NKEYS + top_i[:, :, 1, None, :]).reshape(t, PEER_HEADS, PEER_TOPK * PEER_TOPK)
        best_s, best_j = lax.top_k(cand_s, PEER_TOPK)
        expert = jnp.take_along_axis(cand_i, best_j, axis=-1)
        g = jax.nn.softmax(best_s, axis=-1)
        h = jnp.einsum('td,thkd->thk', xb, u_emb[expert])
        act = (g * jax.nn.gelu(h.astype(jnp.float32), approximate=False)).astype(xb.dtype)
        return jnp.einsum('thk,thkd->td', act, v_emb[expert])

    out = lax.map(block, xt.reshape(nb, PEER_TOK_BLOCK, D))
    return out.reshape(nb * PEER_TOK_BLOCK, D)[:n].reshape(B, T, D)


def hybrid_layer(x, k_past, v_past, logf_past, conv_past, g_mix, w_in, b_f, w_dw, b_dw, ln_g, ln_b,
                 w_out, g_ffn, w_pq, sub_keys, u_emb, v_emb):
    B, T, _ = x.shape
    h = rms_norm(x, g_mix)
    z = h @ w_in
    o1 = D_ATTN; o2 = 2 * D_ATTN; o3 = 3 * D_ATTN; o4 = o3 + N_HEADS; o5 = o4 + D_CONV
    q, k, v, f_logit, a, gate = jnp.split(z, [o1, o2, o3, o4, o5], axis=-1)
    q = q.reshape(B, T, N_HEADS, HEAD_DIM)
    k = k.reshape(B, T, N_HEADS, HEAD_DIM)
    v = v.reshape(B, T, N_HEADS, HEAD_DIM)
    logf = jax.nn.log_sigmoid((f_logit + b_f).astype(jnp.float32))
    attn = fox_attention(q, k, v, logf, k_past, v_past, logf_past)
    conv, conv_state = conv_module(a, gate, conv_past, w_dw, b_dw, ln_g, ln_b)
    x = x + jnp.concatenate([attn, conv], axis=-1) @ w_out
    x = x + peer_ffn(rms_norm(x, g_ffn), w_pq, sub_keys, u_emb, v_emb)
    return x, k, v, logf.astype(x.dtype), conv_state


def setup_inputs(seed: int = 0) -> dict:
    key = jax.random.key(seed)
    ks = jax.random.split(key, 24)
    f32 = jnp.float32
    nrm = lambda i, shape, s: jax.random.normal(ks[i], shape, f32) * s
    return {
        'x_prompt': nrm(0, (BATCH, SEQ, D_MODEL), 1.0),
        'x_sample': nrm(1, (DEC_BATCH, DEC_SEQ, D_MODEL), 1.0),
        'cache_k': nrm(2, (DEPTH, DEC_BATCH, PAST_LEN, N_HEADS, HEAD_DIM), 1.0),
        'cache_v': nrm(3, (DEPTH, DEC_BATCH, PAST_LEN, N_HEADS, HEAD_DIM), 1.0),
        'cache_logf': jax.nn.log_sigmoid(3.0 + nrm(4, (DEPTH, DEC_BATCH, PAST_LEN, N_HEADS), 0.5)),
        'state_conv': nrm(5, (DEPTH, DEC_BATCH, CONV_STATE, D_CONV), 0.5),
        'g_mix': 1.0 + nrm(6, (DEPTH, D_MODEL), 0.01),
        'w_in': nrm(7, (DEPTH, D_MODEL, D_IN_PROJ), D_MODEL ** -0.5),
        'b_f': 3.0 + nrm(8, (DEPTH, N_HEADS), 0.5),
        'w_dw': nrm(9, (DEPTH, CONV_WIDTH, D_CONV), CONV_WIDTH ** -0.5),
        'b_dw': nrm(10, (DEPTH, D_CONV), 0.01),
        'ln_g': 1.0 + nrm(11, (DEPTH, D_CONV), 0.01),
        'ln_b': nrm(12, (DEPTH, D_CONV), 0.01),
        'w_out': nrm(13, (DEPTH, D_MIX, D_MODEL), D_MIX ** -0.5),
        'g_ffn': 1.0 + nrm(14, (DEPTH, D_MODEL), 0.01),
        'w_pq': nrm(15, (DEPTH, D_MODEL, PEER_HEADS * PEER_QDIM), D_MODEL ** -0.5),
        'sub_keys': nrm(16, (DEPTH, 2, PEER_HEADS, PEER_NKEYS, PEER_HALF), PEER_HALF ** -0.5),
        'u_emb': nrm(17, (DEPTH, PEER_EXPERTS, D_MODEL), D_MODEL ** -0.5),
        'v_emb': nrm(18, (DEPTH, PEER_EXPERTS, D_MODEL), PEER_HEADS ** -0.5),
        'g_final': 1.0 + nrm(19, (D_MODEL,), 0.01),
    }


def reference(x_prompt, x_sample, cache_k, cache_v, cache_logf, state_conv, g_mix, w_in, b_f, w_dw, b_dw,
              ln_g, ln_b, w_out, g_ffn, w_pq, sub_keys, u_emb, v_emb, g_final):
    xp, xs = x_prompt, x_sample
    bp = xp.shape[0]
    kp_l, vp_l, fp_l, cp_l, ks_l, vs_l, fs_l, cs_l = [], [], [], [], [], [], [], []
    for l in range(DEPTH):
        w = (g_mix[l], w_in[l], b_f[l], w_dw[l], b_dw[l], ln_g[l], ln_b[l], w_out[l], g_ffn[l],
             w_pq[l], sub_keys[l], u_emb[l], v_emb[l])
        empty_kv = jnp.zeros((bp, 0, N_HEADS, HEAD_DIM), xp.dtype)
        empty_f = jnp.zeros((bp, 0, N_HEADS), jnp.float32)
        zero_conv = jnp.zeros((bp, CONV_STATE, D_CONV), xp.dtype)
        xp, kp, vp, fp, cp = hybrid_layer(xp, empty_kv, empty_kv, empty_f, zero_conv, *w)
        xs, ks_, vs_, fs_, cs_ = hybrid_layer(xs, cache_k[l], cache_v[l], cache_logf[l], state_conv[l], *w)
        kp_l.append(kp); vp_l.append(vp); fp_l.append(fp); cp_l.append(cp)
        ks_l.append(ks_); vs_l.append(vs_); fs_l.append(fs_); cs_l.append(cs_)
    y_prompt = rms_norm(xp, g_final)
    y_sample = rms_norm(xs, g_final)
    return (y_prompt, y_sample,
            jnp.stack(kp_l), jnp.stack(vp_l), jnp.stack(fp_l), jnp.stack(cp_l),
            jnp.stack(ks_l), jnp.stack(vs_l), jnp.stack(fs_l), jnp.stack(cs_l))
```

```python
import functools

import jax
import jax.numpy as jnp
from jax import lax
from jax.experimental import pallas as pl
from jax.experimental.pallas import tpu as pltpu

D_MODEL = 1024
D_ATTN = 512
D_CONV = 512
HEAD_DIM = 64
N_HEADS = 8
CONV_WIDTH = 31
CONV_STATE = CONV_WIDTH - 1
PEER_HEADS = 8
PEER_NKEYS = 128
PEER_EXPERTS = PEER_NKEYS * PEER_NKEYS
PEER_HALF = 128
PEER_TOPK = 16
RMS_EPS = 1e-6
LN_EPS = 1e-5

LANES = 128
CONV_HALO = 32
NEG_BIG = -1e30
VMEM_LIMIT = 56 * 1024 * 1024

F32 = jnp.float32
BF16 = jnp.bfloat16


def _cparams(sem):
    return pltpu.CompilerParams(dimension_semantics=sem, vmem_limit_bytes=VMEM_LIMIT)


def _dot(a, b):
    return jnp.dot(a, b, preferred_element_type=F32)


def _inproj_kernel(x_ref, g_ref, wqkv_ref, wf_ref, wag_ref, bf_ref,
                   q_ref, k_ref, v_ref, kb_ref, vb_ref, logf_ref, u_ref):
    x = x_ref[...]
    ms = jnp.mean(x * x, axis=-1, keepdims=True)
    h = (x * lax.rsqrt(ms + RMS_EPS) * g_ref[...]).astype(BF16)
    qkv = _dot(h, wqkv_ref[...])
    q_ref[...] = (qkv[:, :D_ATTN] * (HEAD_DIM ** -0.5)).astype(BF16)
    k = qkv[:, D_ATTN:2 * D_ATTN]
    v = qkv[:, 2 * D_ATTN:]
    k_ref[...] = k
    v_ref[...] = v
    kb_ref[...] = k.astype(BF16)
    vb_ref[...] = v.astype(BF16)
    f = _dot(h, wf_ref[...]) + bf_ref[...]
    logf_ref[...] = jax.nn.log_sigmoid(f)[:, :N_HEADS]
    ag = _dot(h, wag_ref[...])
    u_ref[...] = ag[:, :D_CONV] * jax.nn.sigmoid(ag[:, D_CONV:])


def _inproj(x2d, g_mix, w_qkv, w_f, w_ag, b_f):
    n = x2d.shape[0]
    tm = min(512, n)
    row = lambda i: (i, 0)
    fixed = lambda i: (0, 0)
    return pl.pallas_call(
        _inproj_kernel,
        grid=(n // tm,),
        in_specs=[
            pl.BlockSpec((tm, D_MODEL), row),
            pl.BlockSpec((1, D_MODEL), fixed),
            pl.BlockSpec((D_MODEL, 3 * D_ATTN), fixed),
            pl.BlockSpec((D_MODEL, LANES), fixed),
            pl.BlockSpec((D_MODEL, 2 * D_CONV), fixed),
            pl.BlockSpec((1, LANES), fixed),
        ],
        out_specs=[
            pl.BlockSpec((tm, D_ATTN), row),
            pl.BlockSpec((tm, D_ATTN), row),
            pl.BlockSpec((tm, D_ATTN), row),
            pl.BlockSpec((tm, D_ATTN), row),
            pl.BlockSpec((tm, D_ATTN), row),
            pl.BlockSpec((tm, N_HEADS), row),
            pl.BlockSpec((tm, D_CONV), row),
        ],
        out_shape=[
            jax.ShapeDtypeStruct((n, D_ATTN), BF16),
            jax.ShapeDtypeStruct((n, D_ATTN), F32),
            jax.ShapeDtypeStruct((n, D_ATTN), F32),
            jax.ShapeDtypeStruct((n, D_ATTN), BF16),
            jax.ShapeDtypeStruct((n, D_ATTN), BF16),
            jax.ShapeDtypeStruct((n, N_HEADS), F32),
            jax.ShapeDtypeStruct((n, D_CONV), F32),
        ],
        compiler_params=_cparams(("arbitrary",)),
        name="inproj",
    )(x2d, g_mix, w_qkv, w_f, w_ag, b_f)


def _cumsum_kernel(lf_ref, tri_ref, c_ref, *, n_chunks):
    tri = tri_ref[...]

    def body(ci, carry):
        start = pl.multiple_of(ci * LANES, LANES)
        x = lf_ref[0, :, pl.ds(start, LANES)]
        x1 = x.astype(BF16)
        r1 = x - x1.astype(F32)
        x2 = r1.astype(BF16)
        x3 = (r1 - x2.astype(F32)).astype(BF16)
        c = _dot(x1, tri) + _dot(x2, tri) + _dot(x3, tri) + carry
        c_ref[0, :, pl.ds(start, LANES)] = c
        return c[:, LANES - 1:LANES]

    lax.fori_loop(0, n_chunks, body, jnp.zeros((N_HEADS, 1), F32))


def _cumsum(lf):
    b, h, l = lf.shape
    tri = (lax.broadcasted_iota(jnp.int32, (LANES, LANES), 0)
           <= lax.broadcasted_iota(jnp.int32, (LANES, LANES), 1)).astype(BF16)
    return pl.pallas_call(
        functools.partial(_cumsum_kernel, n_chunks=l // LANES),
        grid=(b,),
        in_specs=[pl.BlockSpec((1, h, l), lambda i: (i, 0, 0)),
                  pl.BlockSpec((LANES, LANES), lambda i: (0, 0))],
        out_specs=pl.BlockSpec((1, h, l), lambda i: (i, 0, 0)),
        out_shape=jax.ShapeDtypeStruct((b, h, l), F32),
        compiler_params=_cparams(("arbitrary",)),
        name="logf_cumsum",
    )(lf, tri)


def _attn_kernel(q_ref, k_ref, v_ref, cq_ref, ck_ref, o_ref, m_ref, l_ref, acc_ref,
                 *, tq, tk, past, n_k):
    i = pl.program_id(2)
    j = pl.program_id(3)
    last_j = jnp.minimum((past + (i + 1) * tq - 1) // tk, n_k - 1)

    @pl.when(j == 0)
    def _():
        m_ref[...] = jnp.full(m_ref.shape, NEG_BIG, F32)
        l_ref[...] = jnp.zeros(l_ref.shape, F32)
        acc_ref[...] = jnp.zeros(acc_ref.shape, F32)

    @pl.when(j <= last_j)
    def _():
        q = q_ref[0]
        k = k_ref[0]
        v = v_ref[0]
        lane = lax.broadcasted_iota(jnp.int32, (tq, LANES), 1)
        q_pos = past + i * tq + lax.broadcasted_iota(jnp.int32, (tq, tk), 0)
        k_pos = j * tk + lax.broadcasted_iota(jnp.int32, (tq, tk), 1)
        visible = k_pos <= q_pos
        for hh in range(2):
            in_head = (lane >= hh * HEAD_DIM) & (lane < (hh + 1) * HEAD_DIM)
            qh = jnp.where(in_head, q, jnp.zeros_like(q))
            s = lax.dot_general(qh, k, (((1,), (1,)), ((), ())), preferred_element_type=F32)
            s = s + cq_ref[0, 0, :, hh:hh + 1] - ck_ref[0, 0, hh:hh + 1, :]
            s = jnp.where(visible, s, NEG_BIG)
            m_prev = m_ref[hh]
            m_new = jnp.maximum(m_prev, jnp.max(s, axis=-1, keepdims=True))
            alpha = jnp.exp(m_prev - m_new)
            p = jnp.exp(s - m_new[:, :1])
            l_ref[hh] = alpha * l_ref[hh] + jnp.sum(p, axis=-1, keepdims=True)
            acc_ref[hh] = alpha * acc_ref[hh] + _dot(p.astype(BF16), v)
            m_ref[hh] = m_new

    @pl.when(j == last_j)
    def _():
        lane = lax.broadcasted_iota(jnp.int32, (tq, LANES), 1)
        o0 = acc_ref[0] / l_ref[0]
        o1 = acc_ref[1] / l_ref[1]
        o_ref[0] = jnp.where(lane < HEAD_DIM, o0, o1).astype(o_ref.dtype)


def _attention(q, k_all, v_all, cq, ck, *, past, tq, tk):
    b, t, _ = q.shape
    l = k_all.shape[1]
    n_q, n_k = t // tq, l // tk
    n_pairs = N_HEADS // 2

    def kv_block(i, j):
        return jnp.minimum(j, jnp.minimum((past + (i + 1) * tq - 1) // tk, n_k - 1))

    return pl.pallas_call(
        functools.partial(_attn_kernel, tq=tq, tk=tk, past=past, n_k=n_k),
        grid=(b, n_pairs, n_q, n_k),
        in_specs=[
            pl.BlockSpec((1, tq, LANES), lambda bi, hp, i, j: (bi, i, hp)),
            pl.BlockSpec((1, tk, LANES), lambda bi, hp, i, j: (bi, kv_block(i, j), hp)),
            pl.BlockSpec((1, tk, LANES), lambda bi, hp, i, j: (bi, kv_block(i, j), hp)),
            pl.BlockSpec((1, 1, tq, 2), lambda bi, hp, i, j: (bi, hp, i, 0)),
            pl.BlockSpec((1, 1, 2, tk), lambda bi, hp, i, j: (bi, hp, 0, kv_block(i, j))),
        ],
        out_specs=pl.BlockSpec((1, tq, LANES), lambda bi, hp, i, j: (bi, i, hp)),
        out_shape=jax.ShapeDtypeStruct((b, t, D_ATTN), BF16),
        scratch_shapes=[
            pltpu.VMEM((2, tq, LANES), F32),
            pltpu.VMEM((2, tq, LANES), F32),
            pltpu.VMEM((2, tq, LANES), F32),
        ],
        compiler_params=_cparams(("arbitrary", "arbitrary", "arbitrary", "arbitrary")),
        name="fox_attention",
    )(q, k_all, v_all, cq, ck)


CONV_ROWS = 32


def _mixout_kernel(x_ref, attn_ref, u_ref, past_ref, wdw_ref, bdw_ref, lng_ref, lnb_ref,
                   woa_ref, woc_ref, x1_ref, ubuf_ref, conv_ref, *, tt):
    ti = pl.program_id(1)

    @pl.when(ti == 0)
    def _():
        ubuf_ref[0:CONV_HALO, :] = past_ref[0]

    @pl.when(ti > 0)
    def _():
        ubuf_ref[0:CONV_HALO, :] = ubuf_ref[tt:tt + CONV_HALO, :]

    ubuf_ref[CONV_HALO:CONV_HALO + tt, :] = u_ref[0]

    lead = CONV_HALO - CONV_STATE
    for c0 in range(0, tt, CONV_ROWS):
        acc = jnp.zeros((CONV_ROWS, D_CONV), F32)
        for w in range(CONV_WIDTH):
            acc = acc + ubuf_ref[c0 + lead + w:c0 + lead + w + CONV_ROWS, :] * wdw_ref[w:w + 1, :]
        y = acc + bdw_ref[...]
        mu = jnp.mean(y, axis=-1, keepdims=True)
        yc = y - mu
        var = jnp.mean(yc * yc, axis=-1, keepdims=True)
        yn = yc * lax.rsqrt(var + LN_EPS) * lng_ref[...] + lnb_ref[...]
        conv_ref[c0:c0 + CONV_ROWS, :] = (yn * jax.nn.sigmoid(yn)).astype(BF16)

    mixed = _dot(attn_ref[0], woa_ref[...]) + _dot(conv_ref[...], woc_ref[...])
    x1_ref[0] = x_ref[0] + mixed


def _mixout(x, attn, u, past_pad, w_dw, b_dw, ln_g, ln_b, wo_a, wo_c):
    b, t, _ = x.shape
    tt = min(512, t)
    tile = lambda bi, ti: (bi, ti, 0)
    fixed = lambda bi, ti: (0, 0)
    return pl.pallas_call(
        functools.partial(_mixout_kernel, tt=tt),
        grid=(b, t // tt),
        in_specs=[
            pl.BlockSpec((1, tt, D_MODEL), tile),
            pl.BlockSpec((1, tt, D_ATTN), tile),
            pl.BlockSpec((1, tt, D_CONV), tile),
            pl.BlockSpec((1, CONV_HALO, D_CONV), lambda bi, ti: (bi, 0, 0)),
            pl.BlockSpec((CONV_HALO, D_CONV), fixed),
            pl.BlockSpec((1, D_CONV), fixed),
            pl.BlockSpec((1, D_CONV), fixed),
            pl.BlockSpec((1, D_CONV), fixed),
            pl.BlockSpec((D_ATTN, D_MODEL), fixed),
            pl.BlockSpec((D_CONV, D_MODEL), fixed),
        ],
        out_specs=pl.BlockSpec((1, tt, D_MODEL), tile),
        out_shape=jax.ShapeDtypeStruct((b, t, D_MODEL), F32),
        scratch_shapes=[
            pltpu.VMEM((tt + CONV_HALO, D_CONV), F32),
            pltpu.VMEM((tt, D_CONV), BF16),
        ],
        compiler_params=_cparams(("arbitrary", "arbitrary")),
        name="mixout",
    )(x, attn, u, past_pad, w_dw, b_dw, ln_g, ln_b, wo_a, wo_c)


def _top_rows(s, k):
    rows = s.shape[0]
    iota = lax.broadcasted_iota(jnp.int32, s.shape, 0).astype(F32)
    vals, idxs = [], []
    for _ in range(k):
        m = jnp.max(s, axis=0, keepdims=True)
        ix = jnp.min(jnp.where(s == m, iota, float(rows)), axis=0, keepdims=True)
        vals.append(m)
        idxs.append(ix)
        s = jnp.where(iota == ix, -jnp.inf, s)
    return vals, idxs, s


def _route_kernel(x1_ref, g_ref, wpqt_ref, keys_ref, ht_ref, cnt0_ref, e0_ref, rank1_ref, e1_ref,
                  *, tt):
    x = x1_ref[...]
    ms = jnp.mean(x * x, axis=-1, keepdims=True)
    h = x * lax.rsqrt(ms + RMS_EPS) * g_ref[...]
    ht = h.T.astype(BF16)
    ht_ref[...] = ht
    key_iota = lax.broadcasted_iota(jnp.int32, (PEER_NKEYS, tt), 0).astype(F32)
    pair_iota = lax.broadcasted_iota(jnp.int32, (PEER_TOPK, tt), 0).astype(F32)
    for hd in range(PEER_HEADS):
        top_s, top_i = [], []
        for p in range(2):
            r0 = (hd * 2 + p) * PEER_HALF
            qt = _dot(wpqt_ref[r0:r0 + PEER_HALF, :], ht).astype(BF16)
            st = _dot(keys_ref[p, hd], qt)
            vals, idxs, _ = _top_rows(st, PEER_TOPK)
            top_s.append(vals)
            top_i.append(idxs)
        s1 = jnp.concatenate(top_s[1], axis=0)
        cand = jnp.concatenate([top_s[0][a] + s1 for a in range(PEER_TOPK)], axis=0)
        best, _, left = _top_rows(cand, PEER_TOPK)
        z = best[0] * 0.0
        for b in range(PEER_TOPK):
            z = z + jnp.exp(best[b] - best[0])
        inv_z = 1.0 / z
        cnt0 = jnp.zeros((PEER_NKEYS, tt), F32)
        e0 = jnp.zeros((PEER_NKEYS, tt), F32)
        rank1 = jnp.full((PEER_NKEYS, tt), float(PEER_TOPK), F32)
        e1 = jnp.zeros((PEER_NKEYS, tt), F32)
        for a in range(PEER_TOPK):
            taken = left[a * PEER_TOPK:(a + 1) * PEER_TOPK, :] == -jnp.inf
            cnt_a = jnp.sum(jnp.where(taken, 1.0, 0.0), axis=0, keepdims=True)
            hit0 = key_iota == top_i[0][a]
            cnt0 = jnp.where(hit0, cnt_a, cnt0)
            e0 = jnp.where(hit0, jnp.exp(top_s[0][a] - top_s[0][0]), e0)
            hit1 = key_iota == top_i[1][a]
            rank1 = jnp.where(hit1, float(a), rank1)
            e1 = jnp.where(hit1, jnp.exp(top_s[1][a] - top_s[1][0]) * inv_z, e1)
        cnt0_ref[hd] = cnt0
        e0_ref[hd] = e0
        rank1_ref[hd] = rank1
        e1_ref[hd] = e1


def _route(x1_2d, g_ffn, w_pqt, keys):
    n = x1_2d.shape[0]
    tt = min(256, n)
    fac = jax.ShapeDtypeStruct((PEER_HEADS, PEER_NKEYS, n), F32)
    fac_spec = pl.BlockSpec((PEER_HEADS, PEER_NKEYS, tt), lambda i: (0, 0, i))
    return pl.pallas_call(
        functools.partial(_route_kernel, tt=tt),
        grid=(n // tt,),
        in_specs=[
            pl.BlockSpec((tt, D_MODEL), lambda i: (i, 0)),
            pl.BlockSpec((1, D_MODEL), lambda i: (0, 0)),
            pl.BlockSpec((2 * PEER_HEADS * PEER_HALF, D_MODEL), lambda i: (0, 0)),
            pl.BlockSpec((2, PEER_HEADS, PEER_NKEYS, PEER_HALF), lambda i: (0, 0, 0, 0)),
        ],
        out_specs=[pl.BlockSpec((D_MODEL, tt), lambda i: (0, i)), fac_spec, fac_spec, fac_spec, fac_spec],
        out_shape=[jax.ShapeDtypeStruct((D_MODEL, n), BF16), fac, fac, fac, fac],
        compiler_params=_cparams(("arbitrary",)),
        name="peer_route",
    )(x1_2d, g_ffn, w_pqt, keys)


def _expert_kernel(ht_ref, cnt0_ref, e0_ref, rank1_ref, e1_ref, u_ref, vt_ref, x1_ref, gfin_ref,
                   y_ref, acc_ref, act_ref, *, tt, te):
    c = pl.program_id(1)

    @pl.when(c == 0)
    def _():
        acc_ref[...] = jnp.zeros(acc_ref.shape, F32)

    hT = _dot(u_ref[...], ht_ref[...])
    n_i = te // PEER_NKEYS
    for ii in range(n_i):
        i = c * n_i + ii
        gate = jnp.zeros((PEER_NKEYS, tt), F32)
        for hd in range(PEER_HEADS):
            cnt = cnt0_ref[hd, pl.ds(i, 1), :]
            e0 = e0_ref[hd, pl.ds(i, 1), :]
            gate = gate + jnp.where(rank1_ref[hd] < cnt, e1_ref[hd] * e0, 0.0)
        hs = hT[ii * PEER_NKEYS:(ii + 1) * PEER_NKEYS, :]
        gelu = 0.5 * hs * (1.0 + lax.erf(hs * (2.0 ** -0.5)))
        act_ref[ii * PEER_NKEYS:(ii + 1) * PEER_NKEYS, :] = (gate * gelu).astype(BF16)
    acc_ref[...] += _dot(vt_ref[...], act_ref[...])

    @pl.when(c == pl.num_programs(1) - 1)
    def _():
        x2 = x1_ref[...] + acc_ref[...].T
        ms = jnp.mean(x2 * x2, axis=-1, keepdims=True)
        y_ref[...] = x2 * lax.rsqrt(ms + RMS_EPS) * gfin_ref[...]


def _expert(ht, cnt0, e0, rank1, e1, u_emb, v_embt, x1_2d, g_final):
    n = x1_2d.shape[0]
    tt = min(512, n)
    te = 1024
    fac_spec = pl.BlockSpec((PEER_HEADS, PEER_NKEYS, tt), lambda i, c: (0, 0, i))
    return pl.pallas_call(
        functools.partial(_expert_kernel, tt=tt, te=te),
        grid=(n // tt, PEER_EXPERTS // te),
        in_specs=[
            pl.BlockSpec((D_MODEL, tt), lambda i, c: (0, i)),
            fac_spec, fac_spec, fac_spec, fac_spec,
            pl.BlockSpec((te, D_MODEL), lambda i, c: (c, 0)),
            pl.BlockSpec((D_MODEL, te), lambda i, c: (0, c)),
            pl.BlockSpec((tt, D_MODEL), lambda i, c: (i, 0)),
            pl.BlockSpec((1, D_MODEL), lambda i, c: (0, 0)),
        ],
        out_specs=pl.BlockSpec((tt, D_MODEL), lambda i, c: (i, 0)),
        out_shape=jax.ShapeDtypeStruct((n, D_MODEL), F32),
        scratch_shapes=[
            pltpu.VMEM((D_MODEL, tt), F32),
            pltpu.VMEM((te, tt), BF16),
        ],
        compiler_params=_cparams(("arbitrary", "arbitrary")),
        name="peer_expert",
    )(ht, cnt0, e0, rank1, e1, u_emb, v_embt, x1_2d, g_final)


def _pad_time(a, mult):
    pad = (-a.shape[2]) % mult
    return jnp.pad(a, ((0, 0), (0, 0), (0, pad))) if pad else a


def _layer(x, k_past, v_past, logf_past, conv_past, w):
    b, t, _ = x.shape
    n = b * t
    past = 0 if k_past is None else k_past.shape[1]

    q, k, v, kb, vb, logf, u = _inproj(x.reshape(n, D_MODEL), w["g_mix"], w["w_qkv"], w["w_f"],
                                       w["w_ag"], w["b_f"])
    logf = logf.reshape(b, t, N_HEADS)
    kb = kb.reshape(b, t, D_ATTN)
    vb = vb.reshape(b, t, D_ATTN)
    if past:
        lf_all = jnp.concatenate([logf_past, logf], axis=1)
        k_all = jnp.concatenate([k_past.reshape(b, past, D_ATTN).astype(BF16), kb], axis=1)
        v_all = jnp.concatenate([v_past.reshape(b, past, D_ATTN).astype(BF16), vb], axis=1)
    else:
        lf_all, k_all, v_all = logf, kb, vb
    l = past + t
    c = _cumsum(_pad_time(jnp.transpose(lf_all, (0, 2, 1)), LANES))[:, :, :l]
    ck = c.reshape(b, N_HEADS // 2, 2, l)
    cq = jnp.transpose(c[:, :, past:].reshape(b, N_HEADS // 2, 2, t), (0, 1, 3, 2))
    tq = min(512, t)
    tk = tq if past == 0 else l
    attn = _attention(q.reshape(b, t, D_ATTN), k_all, v_all, cq, ck, past=past, tq=tq, tk=tk)

    if conv_past is None:
        past_pad = jnp.zeros((b, CONV_HALO, D_CONV), F32)
    else:
        past_pad = jnp.pad(conv_past, ((0, 0), (CONV_HALO - CONV_STATE, 0), (0, 0)))
    u3 = u.reshape(b, t, D_CONV)
    x1 = _mixout(x, attn, u3, past_pad, w["w_dw"], w["b_dw"], w["ln_g"], w["ln_b"],
                 w["wo_a"], w["wo_c"])
    u_full_tail = u3 if conv_past is None else jnp.concatenate([conv_past, u3], axis=1)
    conv_state = u_full_tail[:, -CONV_STATE:]

    x1_2d = x1.reshape(n, D_MODEL)
    ht, cnt0, e0, rank1, e1 = _route(x1_2d, w["g_ffn"], w["w_pqt"], w["keys"])
    y = _expert(ht, cnt0, e0, rank1, e1, w["u_emb"], w["v_embt"], x1_2d, w["g_final"])
    return (y.reshape(b, t, D_MODEL), k.reshape(b, t, N_HEADS, HEAD_DIM),
            v.reshape(b, t, N_HEADS, HEAD_DIM), logf, conv_state)


def _prep_weights(g_mix, w_in, b_f, w_dw, b_dw, ln_g, ln_b, w_out, g_ffn, w_pq, sub_keys,
                  u_emb, v_emb, g_final):
    o3 = 3 * D_ATTN
    o4 = o3 + N_HEADS
    row = lambda a: a.reshape(1, -1)
    return {
        "g_mix": row(g_mix),
        "w_qkv": w_in[:, :o3].astype(BF16),
        "w_f": jnp.pad(w_in[:, o3:o4], ((0, 0), (0, LANES - N_HEADS))).astype(BF16),
        "w_ag": w_in[:, o4:].astype(BF16),
        "b_f": jnp.pad(b_f, (0, LANES - N_HEADS)).reshape(1, LANES),
        "w_dw": jnp.pad(w_dw, ((0, CONV_HALO - CONV_WIDTH), (0, 0))),
        "b_dw": row(b_dw), "ln_g": row(ln_g), "ln_b": row(ln_b),
        "wo_a": w_out[:D_ATTN].astype(BF16),
        "wo_c": w_out[D_ATTN:].astype(BF16),
        "g_ffn": row(g_ffn),
        "w_pqt": w_pq.T.astype(BF16),
        "keys": sub_keys.astype(BF16),
        "u_emb": u_emb.astype(BF16),
        "v_embt": v_emb.T.astype(BF16),
        "g_final": row(g_final),
    }


def kernel(x_prompt, x_sample, cache_k, cache_v, cache_logf, state_conv, g_mix, w_in, b_f, w_dw, b_dw,
           ln_g, ln_b, w_out, g_ffn, w_pq, sub_keys, u_emb, v_emb, g_final):
    assert g_mix.shape[0] == 1, "single layer"
    w = _prep_weights(g_mix[0], w_in[0], b_f[0], w_dw[0], b_dw[0], ln_g[0], ln_b[0], w_out[0],
                      g_ffn[0], w_pq[0], sub_keys[0], u_emb[0], v_emb[0], g_final)
    yp, kp, vp, fp, cp = _layer(x_prompt, None, None, None, None, w)
    ys, ks, vs, fs, cs = _layer(x_sample, cache_k[0], cache_v[0], cache_logf[0], state_conv[0], w)
    stack = lambda a: a[None]
    return (yp, ys, stack(kp), stack(vp), stack(fp), stack(cp),
            stack(ks), stack(vs), stack(fs), stack(cs))
```

```python
import functools

import jax
import jax.numpy as jnp
from jax import lax
from jax.experimental import pallas as pl
from jax.experimental.pallas import tpu as pltpu

D_MODEL = 1024
D_ATTN = 512
D_CONV = 512
HEAD_DIM = 64
N_HEADS = 8
CONV_WIDTH = 31
CONV_STATE = CONV_WIDTH - 1
PEER_HEADS = 8
PEER_NKEYS = 128
PEER_EXPERTS = PEER_NKEYS * PEER_NKEYS
PEER_HALF = 128
PEER_TOPK = 16
RMS_EPS = 1e-6
LN_EPS = 1e-5

LANES = 128
SUBLANES = 8
CONV_HALO = 32
NEG_BIG = -1e30
AUG_COLS = 8
VMEM_LIMIT = 56 * 1024 * 1024

F32 = jnp.float32
BF16 = jnp.bfloat16


def _cparams(sem):
    return pltpu.CompilerParams(dimension_semantics=sem, vmem_limit_bytes=VMEM_LIMIT)


def _dot(a, b):
    return jnp.dot(a, b, preferred_element_type=F32)


def _dot_nt(a, b):
    return lax.dot_general(a, b, (((1,), (1,)), ((), ())), preferred_element_type=F32)


def _inproj_kernel(x_ref, g_ref, wqkv_ref, wf_ref, wag_ref, bf_ref,
                   q_ref, k_ref, v_ref, kb_ref, vb_ref, logf_ref, u_ref):
    x = x_ref[...]
    ms = jnp.mean(x * x, axis=-1, keepdims=True)
    h = (x * lax.rsqrt(ms + RMS_EPS) * g_ref[...]).astype(BF16)
    qkv = _dot(h, wqkv_ref[...])
    q_ref[...] = (qkv[:, :D_ATTN] * (HEAD_DIM ** -0.5)).astype(BF16)
    k = qkv[:, D_ATTN:2 * D_ATTN]
    v = qkv[:, 2 * D_ATTN:]
    k_ref[...] = k
    v_ref[...] = v
    kb_ref[...] = k.astype(BF16)
    vb_ref[...] = v.astype(BF16)
    f = _dot(h, wf_ref[...]) + bf_ref[...]
    logf_ref[...] = jax.nn.log_sigmoid(f)[:, :N_HEADS]
    ag = _dot(h, wag_ref[...])
    u_ref[...] = ag[:, :D_CONV] * jax.nn.sigmoid(ag[:, D_CONV:])


def _inproj(x2d, g_mix, w_qkv, w_f, w_ag, b_f):
    n = x2d.shape[0]
    tm = min(512, n)
    row = lambda i: (i, 0)
    fixed = lambda i: (0, 0)
    return pl.pallas_call(
        _inproj_kernel,
        grid=(n // tm,),
        in_specs=[
            pl.BlockSpec((tm, D_MODEL), row),
            pl.BlockSpec((1, D_MODEL), fixed),
            pl.BlockSpec((D_MODEL, 3 * D_ATTN), fixed),
            pl.BlockSpec((D_MODEL, LANES), fixed),
            pl.BlockSpec((D_MODEL, 2 * D_CONV), fixed),
            pl.BlockSpec((1, LANES), fixed),
        ],
        out_specs=[
            pl.BlockSpec((tm, D_ATTN), row),
            pl.BlockSpec((tm, D_ATTN), row),
            pl.BlockSpec((tm, D_ATTN), row),
            pl.BlockSpec((tm, D_ATTN), row),
            pl.BlockSpec((tm, D_ATTN), row),
            pl.BlockSpec((tm, N_HEADS), row),
            pl.BlockSpec((tm, D_CONV), row),
        ],
        out_shape=[
            jax.ShapeDtypeStruct((n, D_ATTN), BF16),
            jax.ShapeDtypeStruct((n, D_ATTN), F32),
            jax.ShapeDtypeStruct((n, D_ATTN), F32),
            jax.ShapeDtypeStruct((n, D_ATTN), BF16),
            jax.ShapeDtypeStruct((n, D_ATTN), BF16),
            jax.ShapeDtypeStruct((n, N_HEADS), F32),
            jax.ShapeDtypeStruct((n, D_CONV), F32),
        ],
        compiler_params=_cparams(("arbitrary",)),
        name="inproj",
    )(x2d, g_mix, w_qkv, w_f, w_ag, b_f)


def _split3(x):
    x1 = x.astype(BF16)
    r1 = x - x1.astype(F32)
    x2 = r1.astype(BF16)
    x3 = (r1 - x2.astype(F32)).astype(BF16)
    return x1, x2, x3


def _cumsum_kernel(lf_ref, tri_ref, hi_ref, mid_ref, lo_ref, *, n_chunks):
    tri = tri_ref[...]

    def body(ci, carry):
        start = pl.multiple_of(ci * LANES, LANES)
        x1, x2, x3 = _split3(lf_ref[0, :, pl.ds(start, LANES)])
        c = _dot(x1, tri) + _dot(x2, tri) + _dot(x3, tri) + carry
        c1, c2, c3 = _split3(c)
        hi_ref[0, :, pl.ds(start, LANES)] = c1
        mid_ref[0, :, pl.ds(start, LANES)] = c2
        lo_ref[0, :, pl.ds(start, LANES)] = c3
        return c[:, LANES - 1:LANES]

    lax.fori_loop(0, n_chunks, body, jnp.zeros((N_HEADS, 1), F32))


def _cumsum(lf):
    b, h, l = lf.shape
    tri = (lax.broadcasted_iota(jnp.int32, (LANES, LANES), 0)
           <= lax.broadcasted_iota(jnp.int32, (LANES, LANES), 1)).astype(BF16)
    spec = pl.BlockSpec((1, h, l), lambda i: (i, 0, 0))
    piece = jax.ShapeDtypeStruct((b, h, l), BF16)
    return pl.pallas_call(
        functools.partial(_cumsum_kernel, n_chunks=l // LANES),
        grid=(b,),
        in_specs=[spec, pl.BlockSpec((LANES, LANES), lambda i: (0, 0))],
        out_specs=[spec, spec, spec],
        out_shape=[piece, piece, piece],
        compiler_params=_cparams(("arbitrary",)),
        name="logf_cumsum",
    )(lf, tri)


def _attn_kernel(q_ref, qa_ref, k_ref, ka_ref, v_ref, o_ref, qcat_ref, m_ref, l_ref, acc_ref,
                 *, tq, tk, past, n_k):
    i = pl.program_id(2)
    lane = lax.broadcasted_iota(jnp.int32, (tq, LANES), 1)
    q = q_ref[0]
    qa = qa_ref[0, 0]
    for hh in range(2):
        in_head = (lane >= hh * HEAD_DIM) & (lane < (hh + 1) * HEAD_DIM)
        in_aug = (lane >= hh * AUG_COLS) & (lane < (hh + 1) * AUG_COLS)
        qcat_ref[hh, :, :LANES] = jnp.where(in_head, q, jnp.zeros_like(q))
        qcat_ref[hh, :, LANES:] = jnp.where(in_aug, qa, jnp.zeros_like(qa))
    m_ref[...] = jnp.full(m_ref.shape, NEG_BIG, F32)
    l_ref[...] = jnp.zeros(l_ref.shape, F32)
    acc_ref[...] = jnp.zeros(acc_ref.shape, F32)

    first_q = past + i * tq
    n_full = jnp.minimum((first_q + 1) // tk, n_k)
    n_all = jnp.minimum((first_q + tq - 1) // tk + 1, n_k)

    def chunk(j, masked):
        start = pl.multiple_of(j * tk, tk)
        kcat = jnp.concatenate([k_ref[0, pl.ds(start, tk), :], ka_ref[0, 0, pl.ds(start, tk), :]],
                               axis=1)
        v = v_ref[0, pl.ds(start, tk), :]
        if masked:
            q_pos = first_q + lax.broadcasted_iota(jnp.int32, (tq, tk), 0)
            k_pos = j * tk + lax.broadcasted_iota(jnp.int32, (tq, tk), 1)
            visible = k_pos <= q_pos
        for hh in range(2):
            s = _dot_nt(qcat_ref[hh], kcat)
            if masked:
                s = jnp.where(visible, s, NEG_BIG)
            m_prev = m_ref[hh]
            m_new = jnp.maximum(m_prev, jnp.max(s, axis=-1, keepdims=True))
            alpha = jnp.exp(m_prev - m_new)
            p = jnp.exp(s - m_new[:, :1])
            l_ref[hh] = alpha * l_ref[hh] + jnp.sum(p, axis=-1, keepdims=True)
            acc_ref[hh] = alpha * acc_ref[hh] + _dot(p.astype(BF16), v)
            m_ref[hh] = m_new

    def full_body(j, carry):
        chunk(j, False)
        return carry

    def masked_body(j, carry):
        chunk(j, True)
        return carry

    lax.fori_loop(0, n_full, full_body, 0)
    lax.fori_loop(n_full, n_all, masked_body, 0)

    o0 = acc_ref[0] / l_ref[0]
    o1 = acc_ref[1] / l_ref[1]
    o_ref[0] = jnp.where(lane < HEAD_DIM, o0, o1).astype(o_ref.dtype)


def _attention(q, q_aug, k_all, k_aug, v_all, *, past, tq, tk):
    b, t, _ = q.shape
    l = k_all.shape[1]
    n_pairs = N_HEADS // 2
    return pl.pallas_call(
        functools.partial(_attn_kernel, tq=tq, tk=tk, past=past, n_k=l // tk),
        grid=(b, n_pairs, t // tq),
        in_specs=[
            pl.BlockSpec((1, tq, LANES), lambda bi, hp, i: (bi, i, hp)),
            pl.BlockSpec((1, 1, tq, LANES), lambda bi, hp, i: (bi, hp, i, 0)),
            pl.BlockSpec((1, l, LANES), lambda bi, hp, i: (bi, 0, hp)),
            pl.BlockSpec((1, 1, l, LANES), lambda bi, hp, i: (bi, hp, 0, 0)),
            pl.BlockSpec((1, l, LANES), lambda bi, hp, i: (bi, 0, hp)),
        ],
        out_specs=pl.BlockSpec((1, tq, LANES), lambda bi, hp, i: (bi, i, hp)),
        out_shape=jax.ShapeDtypeStruct((b, t, D_ATTN), BF16),
        scratch_shapes=[
            pltpu.VMEM((2, tq, 2 * LANES), BF16),
            pltpu.VMEM((2, tq, LANES), F32),
            pltpu.VMEM((2, tq, LANES), F32),
            pltpu.VMEM((2, tq, LANES), F32),
        ],
        compiler_params=_cparams(("arbitrary", "arbitrary", "arbitrary")),
        name="fox_attention",
    )(q, q_aug, k_all, k_aug, v_all)


def _bias_columns(first, second):
    b, h, l, _ = first.shape
    cols = jnp.concatenate([first, second, jnp.zeros((b, h, l, AUG_COLS - 6), BF16)], axis=-1)
    cols = jnp.transpose(cols.reshape(b, h // 2, 2, l, AUG_COLS), (0, 1, 3, 2, 4))
    cols = cols.reshape(b, h // 2, l, 2 * AUG_COLS)
    return jnp.pad(cols, ((0, 0), (0, 0), (0, 0), (0, LANES - 2 * AUG_COLS)))


CONV_ROWS = 32


def _mixout_kernel(x_ref, attn_ref, u_ref, past_ref, wdw_ref, bdw_ref, lng_ref, lnb_ref,
                   woa_ref, woc_ref, x1_ref, ubuf_ref, conv_ref, *, tt):
    ti = pl.program_id(1)

    @pl.when(ti == 0)
    def _():
        ubuf_ref[0:CONV_HALO, :] = past_ref[0]

    @pl.when(ti > 0)
    def _():
        ubuf_ref[0:CONV_HALO, :] = ubuf_ref[tt:tt + CONV_HALO, :]

    ubuf_ref[CONV_HALO:CONV_HALO + tt, :] = u_ref[0]

    lead = CONV_HALO - CONV_STATE
    for c0 in range(0, tt, CONV_ROWS):
        acc = jnp.zeros((CONV_ROWS, D_CONV), F32)
        for w in range(CONV_WIDTH):
            acc = acc + ubuf_ref[c0 + lead + w:c0 + lead + w + CONV_ROWS, :] * wdw_ref[w:w + 1, :]
        y = acc + bdw_ref[...]
        mu = jnp.mean(y, axis=-1, keepdims=True)
        yc = y - mu
        var = jnp.mean(yc * yc, axis=-1, keepdims=True)
        yn = yc * lax.rsqrt(var + LN_EPS) * lng_ref[...] + lnb_ref[...]
        conv_ref[c0:c0 + CONV_ROWS, :] = (yn * jax.nn.sigmoid(yn)).astype(BF16)

    mixed = _dot(attn_ref[0], woa_ref[...]) + _dot(conv_ref[...], woc_ref[...])
    x1_ref[0] = x_ref[0] + mixed


def _mixout(x, attn, u, past_pad, w_dw, b_dw, ln_g, ln_b, wo_a, wo_c):
    b, t, _ = x.shape
    tt = min(512, t)
    tile = lambda bi, ti: (bi, ti, 0)
    fixed = lambda bi, ti: (0, 0)
    return pl.pallas_call(
        functools.partial(_mixout_kernel, tt=tt),
        grid=(b, t // tt),
        in_specs=[
            pl.BlockSpec((1, tt, D_MODEL), tile),
            pl.BlockSpec((1, tt, D_ATTN), tile),
            pl.BlockSpec((1, tt, D_CONV), tile),
            pl.BlockSpec((1, CONV_HALO, D_CONV), lambda bi, ti: (bi, 0, 0)),
            pl.BlockSpec((CONV_HALO, D_CONV), fixed),
            pl.BlockSpec((1, D_CONV), fixed),
            pl.BlockSpec((1, D_CONV), fixed),
            pl.BlockSpec((1, D_CONV), fixed),
            pl.BlockSpec((D_ATTN, D_MODEL), fixed),
            pl.BlockSpec((D_CONV, D_MODEL), fixed),
        ],
        out_specs=pl.BlockSpec((1, tt, D_MODEL), tile),
        out_shape=jax.ShapeDtypeStruct((b, t, D_MODEL), F32),
        scratch_shapes=[
            pltpu.VMEM((tt + CONV_HALO, D_CONV), F32),
            pltpu.VMEM((tt, D_CONV), BF16),
        ],
        compiler_params=_cparams(("arbitrary", "arbitrary")),
        name="mixout",
    )(x, attn, u, past_pad, w_dw, b_dw, ln_g, ln_b, wo_a, wo_c)


PAIR_COLS = [PEER_TOPK // (a + 1) for a in range(PEER_TOPK)]


def _pair_rows(a):
    if a == 0:
        return 0, PEER_TOPK
    return PEER_TOPK + (a - 1) * SUBLANES, SUBLANES


def _top_rows(s, k, scatter):
    rows, tt = s.shape
    orig = s
    iota = lax.broadcasted_iota(jnp.int32, s.shape, 0).astype(F32)
    vals = []
    rank = jnp.full(s.shape, float(k), F32) if scatter else None
    for a in range(k):
        m = jnp.max(s, axis=0, keepdims=True)
        ix = jnp.min(jnp.where(s == m, iota, float(rows)), axis=0, keepdims=True)
        onehot = iota == ix
        vals.append(m)
        if scatter:
            rank = jnp.where(onehot, float(a), rank)
        s = jnp.where(onehot, -jnp.inf, s)
    e = jnp.where(s == -jnp.inf, jnp.exp(orig - vals[0]), 0.0) if scatter else None
    return vals, s, rank, e


def _route_kernel(x1_ref, g_ref, wpqt_ref, keys_ref, ht_ref, cnt0_ref, e0_ref, rank1_ref, e1_ref,
                  *, tt):
    x = x1_ref[...]
    ms = jnp.mean(x * x, axis=-1, keepdims=True)
    h = x * lax.rsqrt(ms + RMS_EPS) * g_ref[...]
    ht = h.T.astype(BF16)
    ht_ref[...] = ht
    row8 = lax.broadcasted_iota(jnp.int32, (SUBLANES, tt), 0)
    for hd in range(PEER_HEADS):
        tops, ranks, es = [], [], []
        for p in range(2):
            r0 = (hd * 2 + p) * PEER_HALF
            qt = _dot(wpqt_ref[r0:r0 + PEER_HALF, :], ht).astype(BF16)
            st = _dot(keys_ref[p, hd], qt)
            vals, _, rank, e = _top_rows(st, PEER_TOPK, True)
            tops.append(vals)
            ranks.append(rank)
            es.append(e)
        s1 = jnp.concatenate(tops[1], axis=0)
        blocks = [tops[0][0] + s1]
        for a in range(1, PEER_TOPK):
            blk = tops[0][a] + s1[:SUBLANES]
            if PAIR_COLS[a] < SUBLANES:
                blk = jnp.where(row8 < PAIR_COLS[a], blk, -jnp.inf)
            blocks.append(blk)
        cand = jnp.concatenate(blocks, axis=0)
        best, left, _, _ = _top_rows(cand, PEER_TOPK, False)
        z = jnp.zeros_like(best[0])
        for b in range(PEER_TOPK):
            z = z + jnp.exp(best[b] - best[0])
        cnt0 = jnp.zeros((PEER_NKEYS, tt), F32)
        for a in range(PEER_TOPK):
            start, size = _pair_rows(a)
            taken = left[start:start + size, :] == -jnp.inf
            n_pad = float(size - PAIR_COLS[a])
            cnt_a = jnp.sum(jnp.where(taken, 1.0, 0.0), axis=0, keepdims=True) - n_pad
            cnt0 = jnp.where(ranks[0] == float(a), cnt_a, cnt0)
        cnt0_ref[hd] = cnt0
        e0_ref[hd] = es[0]
        rank1_ref[hd] = ranks[1].astype(rank1_ref.dtype)
        e1_ref[hd] = (es[1] * (1.0 / z)).astype(e1_ref.dtype)


def _route(x1_2d, g_ffn, w_pqt, keys):
    n = x1_2d.shape[0]
    tt = min(256, n)
    fac_shape = (PEER_HEADS, PEER_NKEYS, n)
    fac_spec = pl.BlockSpec((PEER_HEADS, PEER_NKEYS, tt), lambda i: (0, 0, i))
    return pl.pallas_call(
        functools.partial(_route_kernel, tt=tt),
        grid=(n // tt,),
        in_specs=[
            pl.BlockSpec((tt, D_MODEL), lambda i: (i, 0)),
            pl.BlockSpec((1, D_MODEL), lambda i: (0, 0)),
            pl.BlockSpec((2 * PEER_HEADS * PEER_HALF, D_MODEL), lambda i: (0, 0)),
            pl.BlockSpec((2, PEER_HEADS, PEER_NKEYS, PEER_HALF), lambda i: (0, 0, 0, 0)),
        ],
        out_specs=[pl.BlockSpec((D_MODEL, tt), lambda i: (0, i)), fac_spec, fac_spec, fac_spec, fac_spec],
        out_shape=[jax.ShapeDtypeStruct((D_MODEL, n), BF16),
                   jax.ShapeDtypeStruct(fac_shape, F32), jax.ShapeDtypeStruct(fac_shape, F32),
                   jax.ShapeDtypeStruct(fac_shape, BF16), jax.ShapeDtypeStruct(fac_shape, BF16)],
        compiler_params=_cparams(("arbitrary",)),
        name="peer_route",
    )(x1_2d, g_ffn, w_pqt, keys)


EXPERT_SUB = 256


def _expert_kernel(ht_ref, cnt0_ref, e0_ref, rank1_ref, e1_ref, u_ref, vt_ref, x1_ref, gfin_ref,
                   y_ref, acc_ref, act_ref, *, tt, te):
    c = pl.program_id(1)

    @pl.when(c == 0)
    def _():
        acc_ref[...] = jnp.zeros(acc_ref.shape, F32)

    keys_per_sub = EXPERT_SUB // PEER_NKEYS
    for sc in range(te // EXPERT_SUB):
        hT = _dot(u_ref[sc * EXPERT_SUB:(sc + 1) * EXPERT_SUB, :], ht_ref[...])
        for ii in range(keys_per_sub):
            i = c * (te // PEER_NKEYS) + sc * keys_per_sub + ii
            gate = jnp.zeros((PEER_NKEYS, tt), BF16)
            for hd in range(PEER_HEADS):
                cnt = cnt0_ref[hd, pl.ds(i, 1), :].astype(BF16)
                e0 = e0_ref[hd, pl.ds(i, 1), :].astype(BF16)
                picked = jnp.where(rank1_ref[hd] < cnt, e1_ref[hd], jnp.zeros((), BF16))
                gate = gate + picked * e0
            hs = hT[ii * PEER_NKEYS:(ii + 1) * PEER_NKEYS, :]
            gelu = 0.5 * hs * (1.0 + lax.erf(hs * (2.0 ** -0.5)))
            r0 = sc * EXPERT_SUB + ii * PEER_NKEYS
            act_ref[r0:r0 + PEER_NKEYS, :] = gelu.astype(BF16) * gate
    acc_ref[...] += _dot(vt_ref[...], act_ref[...])

    @pl.when(c == pl.num_programs(1) - 1)
    def _():
        x2 = x1_ref[...] + acc_ref[...].T
        ms = jnp.mean(x2 * x2, axis=-1, keepdims=True)
        y_ref[...] = x2 * lax.rsqrt(ms + RMS_EPS) * gfin_ref[...]


def _expert(ht, cnt0, e0, rank1, e1, u_emb, v_embt, x1_2d, g_final):
    n = x1_2d.shape[0]
    tt = min(512, n)
    te = 1024
    fac_spec = pl.BlockSpec((PEER_HEADS, PEER_NKEYS, tt), lambda i, c: (0, 0, i))
    return pl.pallas_call(
        functools.partial(_expert_kernel, tt=tt, te=te),
        grid=(n // tt, PEER_EXPERTS // te),
        in_specs=[
            pl.BlockSpec((D_MODEL, tt), lambda i, c: (0, i)),
            fac_spec, fac_spec, fac_spec, fac_spec,
            pl.BlockSpec((te, D_MODEL), lambda i, c: (c, 0)),
            pl.BlockSpec((D_MODEL, te), lambda i, c: (0, c)),
            pl.BlockSpec((tt, D_MODEL), lambda i, c: (i, 0)),
            pl.BlockSpec((1, D_MODEL), lambda i, c: (0, 0)),
        ],
        out_specs=pl.BlockSpec((tt, D_MODEL), lambda i, c: (i, 0)),
        out_shape=jax.ShapeDtypeStruct((n, D_MODEL), F32),
        scratch_shapes=[
            pltpu.VMEM((D_MODEL, tt), F32),
            pltpu.VMEM((te, tt), BF16),
        ],
        compiler_params=_cparams(("arbitrary", "arbitrary")),
        name="peer_expert",
    )(ht, cnt0, e0, rank1, e1, u_emb, v_embt, x1_2d, g_final)


def _pad_time(a, mult):
    pad = (-a.shape[2]) % mult
    return jnp.pad(a, ((0, 0), (0, 0), (0, pad))) if pad else a


def _layer(x, k_past, v_past, logf_past, conv_past, w):
    b, t, _ = x.shape
    n = b * t
    past = 0 if k_past is None else k_past.shape[1]

    q, k, v, kb, vb, logf, u = _inproj(x.reshape(n, D_MODEL), w["g_mix"], w["w_qkv"], w["w_f"],
                                       w["w_ag"], w["b_f"])
    logf = logf.reshape(b, t, N_HEADS)
    kb = kb.reshape(b, t, D_ATTN)
    vb = vb.reshape(b, t, D_ATTN)
    if past:
        lf_all = jnp.concatenate([logf_past, logf], axis=1)
        k_all = jnp.concatenate([k_past.reshape(b, past, D_ATTN).astype(BF16), kb], axis=1)
        v_all = jnp.concatenate([v_past.reshape(b, past, D_ATTN).astype(BF16), vb], axis=1)
    else:
        lf_all, k_all, v_all = logf, kb, vb
    l = past + t
    pieces = _cumsum(_pad_time(jnp.transpose(lf_all, (0, 2, 1)), LANES))
    pieces = jnp.stack([p[:, :, :l] for p in pieces], axis=-1)
    ones = jnp.ones_like(pieces)
    k_aug = _bias_columns(ones, -pieces)
    q_aug = _bias_columns(pieces[:, :, past:], ones[:, :, past:])
    tq = min(512, t)
    tk = tq if past == 0 else l
    attn = _attention(q.reshape(b, t, D_ATTN), q_aug, k_all, k_aug, v_all, past=past, tq=tq, tk=tk)

    if conv_past is None:
        past_pad = jnp.zeros((b, CONV_HALO, D_CONV), F32)
    else:
        past_pad = jnp.pad(conv_past, ((0, 0), (CONV_HALO - CONV_STATE, 0), (0, 0)))
    u3 = u.reshape(b, t, D_CONV)
    x1 = _mixout(x, attn, u3, past_pad, w["w_dw"], w["b_dw"], w["ln_g"], w["ln_b"],
                 w["wo_a"], w["wo_c"])
    u_full_tail = u3 if conv_past is None else jnp.concatenate([conv_past, u3], axis=1)
    conv_state = u_full_tail[:, -CONV_STATE:]

    x1_2d = x1.reshape(n, D_MODEL)
    ht, cnt0, e0, rank1, e1 = _route(x1_2d, w["g_ffn"], w["w_pqt"], w["keys"])
    y = _expert(ht, cnt0, e0, rank1, e1, w["u_emb"], w["v_embt"], x1_2d, w["g_final"])
    return (y.reshape(b, t, D_MODEL), k.reshape(b, t, N_HEADS, HEAD_DIM),
            v.reshape(b, t, N_HEADS, HEAD_DIM), logf, conv_state)


def _prep_weights(g_mix, w_in, b_f, w_dw, b_dw, ln_g, ln_b, w_out, g_ffn, w_pq, sub_keys,
                  u_emb, v_emb, g_final):
    o3 = 3 * D_ATTN
    o4 = o3 + N_HEADS
    row = lambda a: a.reshape(1, -1)
    return {
        "g_mix": row(g_mix),
        "w_qkv": w_in[:, :o3].astype(BF16),
        "w_f": jnp.pad(w_in[:, o3:o4], ((0, 0), (0, LANES - N_HEADS))).astype(BF16),
        "w_ag": w_in[:, o4:].astype(BF16),
        "b_f": jnp.pad(b_f, (0, LANES - N_HEADS)).reshape(1, LANES),
        "w_dw": jnp.pad(w_dw, ((0, CONV_HALO - CONV_WIDTH), (0, 0))),
        "b_dw": row(b_dw), "ln_g": row(ln_g), "ln_b": row(ln_b),
        "wo_a": w_out[:D_ATTN].astype(BF16),
        "wo_c": w_out[D_ATTN:].astype(BF16),
        "g_ffn": row(g_ffn),
        "w_pqt": w_pq.T.astype(BF16),
        "keys": sub_keys.astype(BF16),
        "u_emb": u_emb.astype(BF16),
        "v_embt": v_emb.T.astype(BF16),
        "g_final": row(g_final),
    }


def kernel(x_prompt, x_sample, cache_k, cache_v, cache_logf, state_conv, g_mix, w_in, b_f, w_dw, b_dw,
           ln_g, ln_b, w_out, g_ffn, w_pq, sub_keys, u_emb, v_emb, g_final):
    assert g_mix.shape[0] == 1, "single layer"
    w = _prep_weights(g_mix[0], w_in[0], b_f[0], w_dw[0], b_dw[0], ln_g[0], ln_b[0], w_out[0],
                      g_ffn[0], w_pq[0], sub_keys[0], u_emb[0], v_emb[0], g_final)
    yp, kp, vp, fp, cp = _layer(x_prompt, None, None, None, None, w)
    ys, ks, vs, fs, cs = _layer(x_sample, cache_k[0], cache_v[0], cache_logf[0], state_conv[0], w)
    stack = lambda a: a[None]
    return (yp, ys, stack(kp), stack(vp), stack(fp), stack(cp),
            stack(ks), stack(vs), stack(fs), stack(cs))
```

```python
import functools

import jax
import jax.numpy as jnp
from jax import lax
from jax.experimental import pallas as pl
from jax.experimental.pallas import tpu as pltpu

D_MODEL = 1024
D_ATTN = 512
D_CONV = 512
HEAD_DIM = 64
N_HEADS = 8
CONV_WIDTH = 31
CONV_STATE = CONV_WIDTH - 1
PEER_HEADS = 8
PEER_NKEYS = 128
PEER_EXPERTS = PEER_NKEYS * PEER_NKEYS
PEER_HALF = 128
PEER_TOPK = 16
RMS_EPS = 1e-6
LN_EPS = 1e-5

LANES = 128
SUBLANES = 8
CONV_HALO = 32
NEG_BIG = -1e30
AUG_COLS = 8
VMEM_LIMIT = 56 * 1024 * 1024

F32 = jnp.float32
BF16 = jnp.bfloat16


def _cparams(sem):
    return pltpu.CompilerParams(dimension_semantics=sem, vmem_limit_bytes=VMEM_LIMIT)


def _dot(a, b):
    return jnp.dot(a, b, preferred_element_type=F32)


def _dot_nt(a, b):
    return lax.dot_general(a, b, (((1,), (1,)), ((), ())), preferred_element_type=F32)


def _inproj_kernel(x_ref, g_ref, wqkv_ref, wf_ref, wag_ref, bf_ref,
                   q_ref, k_ref, v_ref, kb_ref, vb_ref, logf_ref, u_ref):
    x = x_ref[...]
    ms = jnp.mean(x * x, axis=-1, keepdims=True)
    h = (x * lax.rsqrt(ms + RMS_EPS) * g_ref[...]).astype(BF16)
    qkv = _dot(h, wqkv_ref[...])
    q_ref[...] = (qkv[:, :D_ATTN] * (HEAD_DIM ** -0.5)).astype(BF16)
    k = qkv[:, D_ATTN:2 * D_ATTN]
    v = qkv[:, 2 * D_ATTN:]
    k_ref[...] = k
    v_ref[...] = v
    kb_ref[...] = k.astype(BF16)
    vb_ref[...] = v.astype(BF16)
    f = _dot(h, wf_ref[...]) + bf_ref[...]
    logf_ref[...] = jax.nn.log_sigmoid(f)[:, :N_HEADS]
    ag = _dot(h, wag_ref[...])
    u_ref[...] = ag[:, :D_CONV] * jax.nn.sigmoid(ag[:, D_CONV:])


def _inproj(x2d, g_mix, w_qkv, w_f, w_ag, b_f):
    n = x2d.shape[0]
    tm = min(512, n)
    row = lambda i: (i, 0)
    fixed = lambda i: (0, 0)
    return pl.pallas_call(
        _inproj_kernel,
        grid=(n // tm,),
        in_specs=[
            pl.BlockSpec((tm, D_MODEL), row),
            pl.BlockSpec((1, D_MODEL), fixed),
            pl.BlockSpec((D_MODEL, 3 * D_ATTN), fixed),
            pl.BlockSpec((D_MODEL, LANES), fixed),
            pl.BlockSpec((D_MODEL, 2 * D_CONV), fixed),
            pl.BlockSpec((1, LANES), fixed),
        ],
        out_specs=[
            pl.BlockSpec((tm, D_ATTN), row),
            pl.BlockSpec((tm, D_ATTN), row),
            pl.BlockSpec((tm, D_ATTN), row),
            pl.BlockSpec((tm, D_ATTN), row),
            pl.BlockSpec((tm, D_ATTN), row),
            pl.BlockSpec((tm, N_HEADS), row),
            pl.BlockSpec((tm, D_CONV), row),
        ],
        out_shape=[
            jax.ShapeDtypeStruct((n, D_ATTN), BF16),
            jax.ShapeDtypeStruct((n, D_ATTN), F32),
            jax.ShapeDtypeStruct((n, D_ATTN), F32),
            jax.ShapeDtypeStruct((n, D_ATTN), BF16),
            jax.ShapeDtypeStruct((n, D_ATTN), BF16),
            jax.ShapeDtypeStruct((n, N_HEADS), F32),
            jax.ShapeDtypeStruct((n, D_CONV), F32),
        ],
        compiler_params=_cparams(("arbitrary",)),
        name="inproj",
    )(x2d, g_mix, w_qkv, w_f, w_ag, b_f)


def _split3(x):
    x1 = x.astype(BF16)
    r1 = x - x1.astype(F32)
    x2 = r1.astype(BF16)
    x3 = (r1 - x2.astype(F32)).astype(BF16)
    return x1, x2, x3


def _cumsum_kernel(lf_ref, tri_ref, hi_ref, mid_ref, lo_ref, *, n_chunks):
    tri = tri_ref[...]

    def body(ci, carry):
        start = pl.multiple_of(ci * LANES, LANES)
        x1, x2, x3 = _split3(lf_ref[0, :, pl.ds(start, LANES)])
        c = _dot(x1, tri) + _dot(x2, tri) + _dot(x3, tri) + carry
        c1, c2, c3 = _split3(c)
        hi_ref[0, :, pl.ds(start, LANES)] = c1
        mid_ref[0, :, pl.ds(start, LANES)] = c2
        lo_ref[0, :, pl.ds(start, LANES)] = c3
        return c[:, LANES - 1:LANES]

    lax.fori_loop(0, n_chunks, body, jnp.zeros((N_HEADS, 1), F32))


def _cumsum(lf):
    b, h, l = lf.shape
    tri = (lax.broadcasted_iota(jnp.int32, (LANES, LANES), 0)
           <= lax.broadcasted_iota(jnp.int32, (LANES, LANES), 1)).astype(BF16)
    spec = pl.BlockSpec((1, h, l), lambda i: (i, 0, 0))
    piece = jax.ShapeDtypeStruct((b, h, l), BF16)
    return pl.pallas_call(
        functools.partial(_cumsum_kernel, n_chunks=l // LANES),
        grid=(b,),
        in_specs=[spec, pl.BlockSpec((LANES, LANES), lambda i: (0, 0))],
        out_specs=[spec, spec, spec],
        out_shape=[piece, piece, piece],
        compiler_params=_cparams(("arbitrary",)),
        name="logf_cumsum",
    )(lf, tri)


def _attn_kernel(q_ref, qa_ref, k_ref, ka_ref, v_ref, o_ref,
                 qcat_ref, s_ref, p_ref, alpha_ref, m_ref, acc_ref, *, tq, tk, past, n_k):
    i = pl.program_id(2)
    lane = lax.broadcasted_iota(jnp.int32, (tq, LANES), 1)
    q = q_ref[0]
    qa = qa_ref[0, 0]
    for hh in range(2):
        in_head = (lane >= hh * HEAD_DIM) & (lane < (hh + 1) * HEAD_DIM)
        in_aug = (lane >= hh * AUG_COLS) & (lane < (hh + 1) * AUG_COLS)
        qcat_ref[hh, :, :LANES] = jnp.where(in_head, q, jnp.zeros_like(q))
        qcat_ref[hh, :, LANES:] = jnp.where(in_aug, qa, jnp.zeros_like(qa))
    m_ref[...] = jnp.full(m_ref.shape, NEG_BIG, F32)
    acc_ref[...] = jnp.zeros(acc_ref.shape, F32)
    p_ref[1] = jnp.zeros(p_ref.shape[1:], p_ref.dtype)
    alpha_ref[1] = jnp.ones(alpha_ref.shape[1:], F32)

    first_q = past + i * tq
    n_full = jnp.minimum((first_q + 1) // tk, n_k)
    n_all = jnp.minimum((first_q + tq - 1) // tk + 1, n_k)
    n_pairs = n_full // 2
    v_lane = lax.broadcasted_iota(jnp.int32, (tk, LANES), 1)

    def scores(j, slot):
        start = pl.multiple_of(j * tk, tk)
        kcat = jnp.concatenate([k_ref[0, pl.ds(start, tk), :], ka_ref[0, 0, pl.ds(start, tk), :]],
                               axis=1)
        for hh in range(2):
            s_ref[slot, hh] = _dot_nt(qcat_ref[hh], kcat)

    def softmax(slot, visible=None):
        for hh in range(2):
            s = s_ref[slot, hh]
            if visible is not None:
                s = jnp.where(visible, s, NEG_BIG)
            m_prev = m_ref[hh]
            m_new = jnp.maximum(m_prev, jnp.max(s, axis=-1, keepdims=True))
            alpha_ref[slot, hh] = jnp.exp(m_prev - m_new)
            p_ref[slot, hh] = jnp.exp(s - m_new[:, :1]).astype(p_ref.dtype)
            m_ref[hh] = m_new

    def accumulate(j, slot):
        start = pl.multiple_of(j * tk, tk)
        v = v_ref[0, pl.ds(start, tk), :]
        for hh in range(2):
            in_head = (v_lane >= hh * HEAD_DIM) & (v_lane < (hh + 1) * HEAD_DIM)
            vh = jnp.where(in_head, v, jnp.ones_like(v))
            acc_ref[hh] = alpha_ref[slot, hh] * acc_ref[hh] + _dot(p_ref[slot, hh], vh)

    def masked_body(j, carry):
        q_pos = first_q + lax.broadcasted_iota(jnp.int32, (tq, tk), 0)
        k_pos = j * tk + lax.broadcasted_iota(jnp.int32, (tq, tk), 1)
        scores(j, 0)
        softmax(0, k_pos <= q_pos)
        accumulate(j, 0)
        return carry

    def single_body(_, carry):
        scores(n_full - 1, 0)
        softmax(0)
        accumulate(n_full - 1, 0)
        return carry

    def pair_body(u, carry):
        last = 2 * n_pairs - 1
        scores(2 * u + 1, 1)
        softmax(0)
        accumulate(jnp.maximum(2 * u - 1, 0), 1)
        scores(jnp.minimum(2 * u + 2, last), 0)
        softmax(1)
        accumulate(2 * u, 0)
        return carry

    lax.fori_loop(n_full, n_all, masked_body, 0)
    lax.fori_loop(0, n_full % 2, single_body, 0)

    @pl.when(n_pairs > 0)
    def _():
        scores(0, 0)

    lax.fori_loop(0, n_pairs, pair_body, 0)

    @pl.when(n_pairs > 0)
    def _():
        accumulate(2 * n_pairs - 1, 1)

    outs = []
    for hh in range(2):
        acc = acc_ref[hh]
        outs.append(acc / pltpu.roll(acc, HEAD_DIM, 1))
    o_ref[0] = jnp.where(lane < HEAD_DIM, outs[0], outs[1]).astype(o_ref.dtype)


def _attention(q, q_aug, k_all, k_aug, v_all, *, past, tq, tk):
    b, t, _ = q.shape
    l = k_all.shape[1]
    n_pairs = N_HEADS // 2
    return pl.pallas_call(
        functools.partial(_attn_kernel, tq=tq, tk=tk, past=past, n_k=l // tk),
        grid=(b, n_pairs, t // tq),
        in_specs=[
            pl.BlockSpec((1, tq, LANES), lambda bi, hp, i: (bi, i, hp)),
            pl.BlockSpec((1, 1, tq, LANES), lambda bi, hp, i: (bi, hp, i, 0)),
            pl.BlockSpec((1, l, LANES), lambda bi, hp, i: (bi, 0, hp)),
            pl.BlockSpec((1, 1, l, LANES), lambda bi, hp, i: (bi, hp, 0, 0)),
            pl.BlockSpec((1, l, LANES), lambda bi, hp, i: (bi, 0, hp)),
        ],
        out_specs=pl.BlockSpec((1, tq, LANES), lambda bi, hp, i: (bi, i, hp)),
        out_shape=jax.ShapeDtypeStruct((b, t, D_ATTN), BF16),
        scratch_shapes=[
            pltpu.VMEM((2, tq, 2 * LANES), BF16),
            pltpu.VMEM((2, 2, tq, tk), F32),
            pltpu.VMEM((2, 2, tq, tk), BF16),
            pltpu.VMEM((2, 2, tq, LANES), F32),
            pltpu.VMEM((2, tq, LANES), F32),
            pltpu.VMEM((2, tq, LANES), F32),
        ],
        compiler_params=_cparams(("arbitrary", "arbitrary", "arbitrary")),
        name="fox_attention",
    )(q, q_aug, k_all, k_aug, v_all)


def _bias_columns(first, second):
    b, h, l, _ = first.shape
    cols = jnp.concatenate([first, second, jnp.zeros((b, h, l, AUG_COLS - 6), BF16)], axis=-1)
    cols = jnp.transpose(cols.reshape(b, h // 2, 2, l, AUG_COLS), (0, 1, 3, 2, 4))
    cols = cols.reshape(b, h // 2, l, 2 * AUG_COLS)
    return jnp.pad(cols, ((0, 0), (0, 0), (0, 0), (0, LANES - 2 * AUG_COLS)))


CONV_ROWS = 32


def _mixout_kernel(x_ref, attn_ref, u_ref, past_ref, wdw_ref, bdw_ref, lng_ref, lnb_ref,
                   woa_ref, woc_ref, x1_ref, ubuf_ref, conv_ref, *, tt):
    ti = pl.program_id(1)

    @pl.when(ti == 0)
    def _():
        ubuf_ref[0:CONV_HALO, :] = past_ref[0]

    @pl.when(ti > 0)
    def _():
        ubuf_ref[0:CONV_HALO, :] = ubuf_ref[tt:tt + CONV_HALO, :]

    ubuf_ref[CONV_HALO:CONV_HALO + tt, :] = u_ref[0]

    lead = CONV_HALO - CONV_STATE
    for c0 in range(0, tt, CONV_ROWS):
        acc = jnp.zeros((CONV_ROWS, D_CONV), F32)
        for w in range(CONV_WIDTH):
            acc = acc + ubuf_ref[c0 + lead + w:c0 + lead + w + CONV_ROWS, :] * wdw_ref[w:w + 1, :]
        y = acc + bdw_ref[...]
        mu = jnp.mean(y, axis=-1, keepdims=True)
        yc = y - mu
        var = jnp.mean(yc * yc, axis=-1, keepdims=True)
        yn = yc * lax.rsqrt(var + LN_EPS) * lng_ref[...] + lnb_ref[...]
        conv_ref[c0:c0 + CONV_ROWS, :] = (yn * jax.nn.sigmoid(yn)).astype(BF16)

    mixed = _dot(attn_ref[0], woa_ref[...]) + _dot(conv_ref[...], woc_ref[...])
    x1_ref[0] = x_ref[0] + mixed


def _mixout(x, attn, u, past_pad, w_dw, b_dw, ln_g, ln_b, wo_a, wo_c):
    b, t, _ = x.shape
    tt = min(512, t)
    tile = lambda bi, ti: (bi, ti, 0)
    fixed = lambda bi, ti: (0, 0)
    return pl.pallas_call(
        functools.partial(_mixout_kernel, tt=tt),
        grid=(b, t // tt),
        in_specs=[
            pl.BlockSpec((1, tt, D_MODEL), tile),
            pl.BlockSpec((1, tt, D_ATTN), tile),
            pl.BlockSpec((1, tt, D_CONV), tile),
            pl.BlockSpec((1, CONV_HALO, D_CONV), lambda bi, ti: (bi, 0, 0)),
            pl.BlockSpec((CONV_HALO, D_CONV), fixed),
            pl.BlockSpec((1, D_CONV), fixed),
            pl.BlockSpec((1, D_CONV), fixed),
            pl.BlockSpec((1, D_CONV), fixed),
            pl.BlockSpec((D_ATTN, D_MODEL), fixed),
            pl.BlockSpec((D_CONV, D_MODEL), fixed),
        ],
        out_specs=pl.BlockSpec((1, tt, D_MODEL), tile),
        out_shape=jax.ShapeDtypeStruct((b, t, D_MODEL), F32),
        scratch_shapes=[
            pltpu.VMEM((tt + CONV_HALO, D_CONV), F32),
            pltpu.VMEM((tt, D_CONV), BF16),
        ],
        compiler_params=_cparams(("arbitrary", "arbitrary")),
        name="mixout",
    )(x, attn, u, past_pad, w_dw, b_dw, ln_g, ln_b, wo_a, wo_c)


PAIR_COLS = [PEER_TOPK // (a + 1) for a in range(PEER_TOPK)]


def _pair_rows(a):
    if a == 0:
        return 0, PEER_TOPK
    return PEER_TOPK + (a - 1) * SUBLANES, SUBLANES


def _top_rows(s, k, scatter):
    rows, tt = s.shape
    orig = s
    iota = lax.broadcasted_iota(jnp.int32, s.shape, 0).astype(F32)
    vals = []
    rank = jnp.full(s.shape, float(k), F32) if scatter else None
    for a in range(k):
        m = jnp.max(s, axis=0, keepdims=True)
        ix = jnp.min(jnp.where(s == m, iota, float(rows)), axis=0, keepdims=True)
        onehot = iota == ix
        vals.append(m)
        if scatter:
            rank = jnp.where(onehot, float(a), rank)
        s = jnp.where(onehot, -jnp.inf, s)
    e = jnp.where(s == -jnp.inf, jnp.exp(orig - vals[0]), 0.0) if scatter else None
    return vals, s, rank, e


def _route_kernel(x1_ref, g_ref, wpqt_ref, keys_ref, ht_ref, cnt0_ref, e0_ref, rank1_ref, e1_ref,
                  *, tt):
    x = x1_ref[...]
    ms = jnp.mean(x * x, axis=-1, keepdims=True)
    h = x * lax.rsqrt(ms + RMS_EPS) * g_ref[...]
    ht = h.T.astype(BF16)
    ht_ref[...] = ht
    row8 = lax.broadcasted_iota(jnp.int32, (SUBLANES, tt), 0)
    for hd in range(PEER_HEADS):
        tops, ranks, es = [], [], []
        for p in range(2):
            r0 = (hd * 2 + p) * PEER_HALF
            qt = _dot(wpqt_ref[r0:r0 + PEER_HALF, :], ht).astype(BF16)
            st = _dot(keys_ref[p, hd], qt)
            vals, _, rank, e = _top_rows(st, PEER_TOPK, True)
            tops.append(vals)
            ranks.append(rank)
            es.append(e)
        s1 = jnp.concatenate(tops[1], axis=0)
        blocks = [tops[0][0] + s1]
        for a in range(1, PEER_TOPK):
            blk = tops[0][a] + s1[:SUBLANES]
            if PAIR_COLS[a] < SUBLANES:
                blk = jnp.where(row8 < PAIR_COLS[a], blk, -jnp.inf)
            blocks.append(blk)
        cand = jnp.concatenate(blocks, axis=0)
        best, left, _, _ = _top_rows(cand, PEER_TOPK, False)
        z = jnp.zeros_like(best[0])
        for b in range(PEER_TOPK):
            z = z + jnp.exp(best[b] - best[0])
        cnt0 = jnp.zeros((PEER_NKEYS, tt), F32)
        for a in range(PEER_TOPK):
            start, size = _pair_rows(a)
            taken = left[start:start + size, :] == -jnp.inf
            n_pad = float(size - PAIR_COLS[a])
            cnt_a = jnp.sum(jnp.where(taken, 1.0, 0.0), axis=0, keepdims=True) - n_pad
            cnt0 = jnp.where(ranks[0] == float(a), cnt_a, cnt0)
        cnt0_ref[hd] = cnt0
        e0_ref[hd] = es[0]
        rank1_ref[hd] = ranks[1].astype(rank1_ref.dtype)
        e1_ref[hd] = (es[1] * (1.0 / z)).astype(e1_ref.dtype)


def _route(x1_2d, g_ffn, w_pqt, keys):
    n = x1_2d.shape[0]
    tt = min(256, n)
    fac_shape = (PEER_HEADS, PEER_NKEYS, n)
    fac_spec = pl.BlockSpec((PEER_HEADS, PEER_NKEYS, tt), lambda i: (0, 0, i))
    return pl.pallas_call(
        functools.partial(_route_kernel, tt=tt),
        grid=(n // tt,),
        in_specs=[
            pl.BlockSpec((tt, D_MODEL), lambda i: (i, 0)),
            pl.BlockSpec((1, D_MODEL), lambda i: (0, 0)),
            pl.BlockSpec((2 * PEER_HEADS * PEER_HALF, D_MODEL), lambda i: (0, 0)),
            pl.BlockSpec((2, PEER_HEADS, PEER_NKEYS, PEER_HALF), lambda i: (0, 0, 0, 0)),
        ],
        out_specs=[pl.BlockSpec((D_MODEL, tt), lambda i: (0, i)), fac_spec, fac_spec, fac_spec, fac_spec],
        out_shape=[jax.ShapeDtypeStruct((D_MODEL, n), BF16),
                   jax.ShapeDtypeStruct(fac_shape, F32), jax.ShapeDtypeStruct(fac_shape, F32),
                   jax.ShapeDtypeStruct(fac_shape, BF16), jax.ShapeDtypeStruct(fac_shape, BF16)],
        compiler_params=_cparams(("arbitrary",)),
        name="peer_route",
    )(x1_2d, g_ffn, w_pqt, keys)


EXPERT_SUB = 256


def _expert_kernel(ht_ref, cnt0_ref, e0_ref, rank1_ref, e1_ref, u_ref, vt_ref, x1_ref, gfin_ref,
                   y_ref, acc_ref, act_ref, *, tt, te):
    c = pl.program_id(1)

    @pl.when(c == 0)
    def _():
        acc_ref[...] = jnp.zeros(acc_ref.shape, F32)

    keys_per_sub = EXPERT_SUB // PEER_NKEYS
    for sc in range(te // EXPERT_SUB):
        hT = _dot(u_ref[sc * EXPERT_SUB:(sc + 1) * EXPERT_SUB, :], ht_ref[...])
        for ii in range(keys_per_sub):
            i = c * (te // PEER_NKEYS) + sc * keys_per_sub + ii
            gate = jnp.zeros((PEER_NKEYS, tt), BF16)
            for hd in range(PEER_HEADS):
                cnt = cnt0_ref[hd, pl.ds(i, 1), :].astype(BF16)
                e0 = e0_ref[hd, pl.ds(i, 1), :].astype(BF16)
                picked = jnp.where(rank1_ref[hd] < cnt, e1_ref[hd], jnp.zeros((), BF16))
                gate = gate + picked * e0
            hs = hT[ii * PEER_NKEYS:(ii + 1) * PEER_NKEYS, :]
            gelu = 0.5 * hs * (1.0 + lax.erf(hs * (2.0 ** -0.5)))
            r0 = sc * EXPERT_SUB + ii * PEER_NKEYS
            act_ref[r0:r0 + PEER_NKEYS, :] = gelu.astype(BF16) * gate
    acc_ref[...] += _dot(vt_ref[...], act_ref[...])

    @pl.when(c == pl.num_programs(1) - 1)
    def _():
        x2 = x1_ref[...] + acc_ref[...].T
        ms = jnp.mean(x2 * x2, axis=-1, keepdims=True)
        y_ref[...] = x2 * lax.rsqrt(ms + RMS_EPS) * gfin_ref[...]


def _expert(ht, cnt0, e0, rank1, e1, u_emb, v_embt, x1_2d, g_final):
    n = x1_2d.shape[0]
    tt = min(512, n)
    te = 1024
    fac_spec = pl.BlockSpec((PEER_HEADS, PEER_NKEYS, tt), lambda i, c: (0, 0, i))
    return pl.pallas_call(
        functools.partial(_expert_kernel, tt=tt, te=te),
        grid=(n // tt, PEER_EXPERTS // te),
        in_specs=[
            pl.BlockSpec((D_MODEL, tt), lambda i, c: (0, i)),
            fac_spec, fac_spec, fac_spec, fac_spec,
            pl.BlockSpec((te, D_MODEL), lambda i, c: (c, 0)),
            pl.BlockSpec((D_MODEL, te), lambda i, c: (0, c)),
            pl.BlockSpec((tt, D_MODEL), lambda i, c: (i, 0)),
            pl.BlockSpec((1, D_MODEL), lambda i, c: (0, 0)),
        ],
        out_specs=pl.BlockSpec((tt, D_MODEL), lambda i, c: (i, 0)),
        out_shape=jax.ShapeDtypeStruct((n, D_MODEL), F32),
        scratch_shapes=[
            pltpu.VMEM((D_MODEL, tt), F32),
            pltpu.VMEM((te, tt), BF16),
        ],
        compiler_params=_cparams(("arbitrary", "arbitrary")),
        name="peer_expert",
    )(ht, cnt0, e0, rank1, e1, u_emb, v_embt, x1_2d, g_final)


def _pad_time(a, mult):
    pad = (-a.shape[2]) % mult
    return jnp.pad(a, ((0, 0), (0, 0), (0, pad))) if pad else a


def _layer(x, k_past, v_past, logf_past, conv_past, w):
    b, t, _ = x.shape
    n = b * t
    past = 0 if k_past is None else k_past.shape[1]

    q, k, v, kb, vb, logf, u = _inproj(x.reshape(n, D_MODEL), w["g_mix"], w["w_qkv"], w["w_f"],
                                       w["w_ag"], w["b_f"])
    logf = logf.reshape(b, t, N_HEADS)
    kb = kb.reshape(b, t, D_ATTN)
    vb = vb.reshape(b, t, D_ATTN)
    if past:
        lf_all = jnp.concatenate([logf_past, logf], axis=1)
        k_all = jnp.concatenate([k_past.reshape(b, past, D_ATTN).astype(BF16), kb], axis=1)
        v_all = jnp.concatenate([v_past.reshape(b, past, D_ATTN).astype(BF16), vb], axis=1)
    else:
        lf_all, k_all, v_all = logf, kb, vb
    l = past + t
    pieces = _cumsum(_pad_time(jnp.transpose(lf_all, (0, 2, 1)), LANES))
    pieces = jnp.stack([p[:, :, :l] for p in pieces], axis=-1)
    ones = jnp.ones_like(pieces)
    k_aug = _bias_columns(ones, -pieces)
    q_aug = _bias_columns(pieces[:, :, past:], ones[:, :, past:])
    tq = min(512, t)
    tk = tq if past == 0 else l
    attn = _attention(q.reshape(b, t, D_ATTN), q_aug, k_all, k_aug, v_all, past=past, tq=tq, tk=tk)

    if conv_past is None:
        past_pad = jnp.zeros((b, CONV_HALO, D_CONV), F32)
    else:
        past_pad = jnp.pad(conv_past, ((0, 0), (CONV_HALO - CONV_STATE, 0), (0, 0)))
    u3 = u.reshape(b, t, D_CONV)
    x1 = _mixout(x, attn, u3, past_pad, w["w_dw"], w["b_dw"], w["ln_g"], w["ln_b"],
                 w["wo_a"], w["wo_c"])
    u_full_tail = u3 if conv_past is None else jnp.concatenate([conv_past, u3], axis=1)
    conv_state = u_full_tail[:, -CONV_STATE:]

    x1_2d = x1.reshape(n, D_MODEL)
    ht, cnt0, e0, rank1, e1 = _route(x1_2d, w["g_ffn"], w["w_pqt"], w["keys"])
    y = _expert(ht, cnt0, e0, rank1, e1, w["u_emb"], w["v_embt"], x1_2d, w["g_final"])
    return (y.reshape(b, t, D_MODEL), k.reshape(b, t, N_HEADS, HEAD_DIM),
            v.reshape(b, t, N_HEADS, HEAD_DIM), logf, conv_state)


def _prep_weights(g_mix, w_in, b_f, w_dw, b_dw, ln_g, ln_b, w_out, g_ffn, w_pq, sub_keys,
                  u_emb, v_emb, g_final):
    o3 = 3 * D_ATTN
    o4 = o3 + N_HEADS
    row = lambda a: a.reshape(1, -1)
    return {
        "g_mix": row(g_mix),
        "w_qkv": w_in[:, :o3].astype(BF16),
        "w_f": jnp.pad(w_in[:, o3:o4], ((0, 0), (0, LANES - N_HEADS))).astype(BF16),
        "w_ag": w_in[:, o4:].astype(BF16),
        "b_f": jnp.pad(b_f, (0, LANES - N_HEADS)).reshape(1, LANES),
        "w_dw": jnp.pad(w_dw, ((0, CONV_HALO - CONV_WIDTH), (0, 0))),
        "b_dw": row(b_dw), "ln_g": row(ln_g), "ln_b": row(ln_b),
        "wo_a": w_out[:D_ATTN].astype(BF16),
        "wo_c": w_out[D_ATTN:].astype(BF16),
        "g_ffn": row(g_ffn),
        "w_pqt": w_pq.T.astype(BF16),
        "keys": sub_keys.astype(BF16),
        "u_emb": u_emb.astype(BF16),
        "v_embt": v_emb.T.astype(BF16),
        "g_final": row(g_final),
    }


def kernel(x_prompt, x_sample, cache_k, cache_v, cache_logf, state_conv, g_mix, w_in, b_f, w_dw, b_dw,
           ln_g, ln_b, w_out, g_ffn, w_pq, sub_keys, u_emb, v_emb, g_final):
    assert g_mix.shape[0] == 1, "single layer"
    w = _prep_weights(g_mix[0], w_in[0], b_f[0], w_dw[0], b_dw[0], ln_g[0], ln_b[0], w_out[0],
                      g_ffn[0], w_pq[0], sub_keys[0], u_emb[0], v_emb[0], g_final)
    yp, kp, vp, fp, cp = _layer(x_prompt, None, None, None, None, w)
    ys, ks, vs, fs, cs = _layer(x_sample, cache_k[0], cache_v[0], cache_logf[0], state_conv[0], w)
    stack = lambda a: a[None]
    return (yp, ys, stack(kp), stack(vp), stack(fp), stack(cp),
            stack(ks), stack(vs), stack(fs), stack(cs))
```

```python
import functools

import jax
import jax.numpy as jnp
from jax import lax
from jax.experimental import pallas as pl
from jax.experimental.pallas import tpu as pltpu

D_MODEL = 1024
D_ATTN = 512
D_CONV = 512
HEAD_DIM = 64
N_HEADS = 8
CONV_WIDTH = 31
CONV_STATE = CONV_WIDTH - 1
PEER_HEADS = 8
PEER_NKEYS = 128
PEER_EXPERTS = PEER_NKEYS * PEER_NKEYS
PEER_HALF = 128
PEER_TOPK = 16
RMS_EPS = 1e-6
LN_EPS = 1e-5

LANES = 128
SUBLANES = 8
CONV_HALO = 32
NEG_BIG = -1e30
AUG_COLS = 8
VMEM_LIMIT = 56 * 1024 * 1024

F32 = jnp.float32
BF16 = jnp.bfloat16


def _cparams(sem):
    return pltpu.CompilerParams(dimension_semantics=sem, vmem_limit_bytes=VMEM_LIMIT)


def _dot(a, b):
    return jnp.dot(a, b, preferred_element_type=F32)


def _dot_nt(a, b):
    return lax.dot_general(a, b, (((1,), (1,)), ((), ())), preferred_element_type=F32)


def _inproj_kernel(x_ref, g_ref, wqkv_ref, wf_ref, wag_ref, bf_ref,
                   q_ref, k_ref, v_ref, kb_ref, vb_ref, logf_ref, u_ref):
    x = x_ref[...]
    ms = jnp.mean(x * x, axis=-1, keepdims=True)
    h = (x * lax.rsqrt(ms + RMS_EPS) * g_ref[...]).astype(BF16)
    qkv = _dot(h, wqkv_ref[...])
    q_ref[...] = (qkv[:, :D_ATTN] * (HEAD_DIM ** -0.5)).astype(BF16)
    k = qkv[:, D_ATTN:2 * D_ATTN]
    v = qkv[:, 2 * D_ATTN:]
    k_ref[...] = k
    v_ref[...] = v
    kb_ref[...] = k.astype(BF16)
    vb_ref[...] = v.astype(BF16)
    f = _dot(h, wf_ref[...]) + bf_ref[...]
    logf_ref[...] = jax.nn.log_sigmoid(f)[:, :N_HEADS]
    ag = _dot(h, wag_ref[...])
    u_ref[...] = ag[:, :D_CONV] * jax.nn.sigmoid(ag[:, D_CONV:])


def _inproj(x2d, g_mix, w_qkv, w_f, w_ag, b_f):
    n = x2d.shape[0]
    tm = min(512, n)
    row = lambda i: (i, 0)
    fixed = lambda i: (0, 0)
    return pl.pallas_call(
        _inproj_kernel,
        grid=(n // tm,),
        in_specs=[
            pl.BlockSpec((tm, D_MODEL), row),
            pl.BlockSpec((1, D_MODEL), fixed),
            pl.BlockSpec((D_MODEL, 3 * D_ATTN), fixed),
            pl.BlockSpec((D_MODEL, LANES), fixed),
            pl.BlockSpec((D_MODEL, 2 * D_CONV), fixed),
            pl.BlockSpec((1, LANES), fixed),
        ],
        out_specs=[
            pl.BlockSpec((tm, D_ATTN), row),
            pl.BlockSpec((tm, D_ATTN), row),
            pl.BlockSpec((tm, D_ATTN), row),
            pl.BlockSpec((tm, D_ATTN), row),
            pl.BlockSpec((tm, D_ATTN), row),
            pl.BlockSpec((tm, N_HEADS), row),
            pl.BlockSpec((tm, D_CONV), row),
        ],
        out_shape=[
            jax.ShapeDtypeStruct((n, D_ATTN), BF16),
            jax.ShapeDtypeStruct((n, D_ATTN), F32),
            jax.ShapeDtypeStruct((n, D_ATTN), F32),
            jax.ShapeDtypeStruct((n, D_ATTN), BF16),
            jax.ShapeDtypeStruct((n, D_ATTN), BF16),
            jax.ShapeDtypeStruct((n, N_HEADS), F32),
            jax.ShapeDtypeStruct((n, D_CONV), F32),
        ],
        compiler_params=_cparams(("arbitrary",)),
        name="inproj",
    )(x2d, g_mix, w_qkv, w_f, w_ag, b_f)


def _split3(x):
    x1 = x.astype(BF16)
    r1 = x - x1.astype(F32)
    x2 = r1.astype(BF16)
    x3 = (r1 - x2.astype(F32)).astype(BF16)
    return x1, x2, x3


def _cumsum_kernel(lf_ref, tri_ref, hi_ref, mid_ref, lo_ref, *, n_chunks):
    tri = tri_ref[...]

    def body(ci, carry):
        start = pl.multiple_of(ci * LANES, LANES)
        x1, x2, x3 = _split3(lf_ref[0, :, pl.ds(start, LANES)])
        c = _dot(x1, tri) + _dot(x2, tri) + _dot(x3, tri) + carry
        c1, c2, c3 = _split3(c)
        hi_ref[0, :, pl.ds(start, LANES)] = c1
        mid_ref[0, :, pl.ds(start, LANES)] = c2
        lo_ref[0, :, pl.ds(start, LANES)] = c3
        return c[:, LANES - 1:LANES]

    lax.fori_loop(0, n_chunks, body, jnp.zeros((N_HEADS, 1), F32))


def _cumsum(lf):
    b, h, l = lf.shape
    tri = (lax.broadcasted_iota(jnp.int32, (LANES, LANES), 0)
           <= lax.broadcasted_iota(jnp.int32, (LANES, LANES), 1)).astype(BF16)
    spec = pl.BlockSpec((1, h, l), lambda i: (i, 0, 0))
    piece = jax.ShapeDtypeStruct((b, h, l), BF16)
    return pl.pallas_call(
        functools.partial(_cumsum_kernel, n_chunks=l // LANES),
        grid=(b,),
        in_specs=[spec, pl.BlockSpec((LANES, LANES), lambda i: (0, 0))],
        out_specs=[spec, spec, spec],
        out_shape=[piece, piece, piece],
        compiler_params=_cparams(("arbitrary",)),
        name="logf_cumsum",
    )(lf, tri)


def _attn_kernel(q_ref, qa_ref, k_ref, ka_ref, v_ref, o_ref,
                 qcat_ref, s_ref, p_ref, alpha_ref, m_ref, acc_ref, *, tq, tk, past, n_k):
    i = pl.program_id(2)
    lane = lax.broadcasted_iota(jnp.int32, (tq, LANES), 1)
    q = q_ref[0]
    qa = qa_ref[0, 0]
    for hh in range(2):
        in_head = (lane >= hh * HEAD_DIM) & (lane < (hh + 1) * HEAD_DIM)
        in_aug = (lane >= hh * AUG_COLS) & (lane < (hh + 1) * AUG_COLS)
        qcat_ref[hh, :, :LANES] = jnp.where(in_head, q, jnp.zeros_like(q))
        qcat_ref[hh, :, LANES:] = jnp.where(in_aug, qa, jnp.zeros_like(qa))
    m_ref[...] = jnp.full(m_ref.shape, NEG_BIG, F32)
    acc_ref[...] = jnp.zeros(acc_ref.shape, F32)
    p_ref[1] = jnp.zeros(p_ref.shape[1:], p_ref.dtype)
    alpha_ref[1] = jnp.ones(alpha_ref.shape[1:], F32)

    first_q = past + i * tq
    n_full = jnp.minimum((first_q + 1) // tk, n_k)
    n_all = jnp.minimum((first_q + tq - 1) // tk + 1, n_k)
    n_pairs = n_full // 2
    v_lane = lax.broadcasted_iota(jnp.int32, (tk, LANES), 1)

    def scores(j, slot):
        start = pl.multiple_of(j * tk, tk)
        kcat = jnp.concatenate([k_ref[0, pl.ds(start, tk), :], ka_ref[0, 0, pl.ds(start, tk), :]],
                               axis=1)
        for hh in range(2):
            s_ref[slot, hh] = _dot_nt(qcat_ref[hh], kcat)

    def softmax(slot, visible=None):
        for hh in range(2):
            s = s_ref[slot, hh]
            if visible is not None:
                s = jnp.where(visible, s, NEG_BIG)
            m_prev = m_ref[hh]
            m_new = jnp.maximum(m_prev, jnp.max(s, axis=-1, keepdims=True))
            alpha_ref[slot, hh] = jnp.exp(m_prev - m_new)
            p_ref[slot, hh] = jnp.exp(s - m_new[:, :1]).astype(p_ref.dtype)
            m_ref[hh] = m_new

    def accumulate(j, slot):
        start = pl.multiple_of(j * tk, tk)
        v = v_ref[0, pl.ds(start, tk), :]
        for hh in range(2):
            in_head = (v_lane >= hh * HEAD_DIM) & (v_lane < (hh + 1) * HEAD_DIM)
            vh = jnp.where(in_head, v, jnp.ones_like(v))
            acc_ref[hh] = alpha_ref[slot, hh] * acc_ref[hh] + _dot(p_ref[slot, hh], vh)

    def masked_body(j, carry):
        q_pos = first_q + lax.broadcasted_iota(jnp.int32, (tq, tk), 0)
        k_pos = j * tk + lax.broadcasted_iota(jnp.int32, (tq, tk), 1)
        scores(j, 0)
        softmax(0, k_pos <= q_pos)
        accumulate(j, 0)
        return carry

    def single_body(_, carry):
        scores(n_full - 1, 0)
        softmax(0)
        accumulate(n_full - 1, 0)
        return carry

    def pair_body(u, carry):
        last = 2 * n_pairs - 1
        scores(2 * u + 1, 1)
        softmax(0)
        accumulate(jnp.maximum(2 * u - 1, 0), 1)
        scores(jnp.minimum(2 * u + 2, last), 0)
        softmax(1)
        accumulate(2 * u, 0)
        return carry

    lax.fori_loop(n_full, n_all, masked_body, 0)
    lax.fori_loop(0, n_full % 2, single_body, 0)

    @pl.when(n_pairs > 0)
    def _():
        scores(0, 0)

    lax.fori_loop(0, n_pairs, pair_body, 0)

    @pl.when(n_pairs > 0)
    def _():
        accumulate(2 * n_pairs - 1, 1)

    outs = []
    for hh in range(2):
        acc = acc_ref[hh]
        outs.append(acc / pltpu.roll(acc, HEAD_DIM, 1))
    o_ref[0] = jnp.where(lane < HEAD_DIM, outs[0], outs[1]).astype(o_ref.dtype)


def _attention(q, q_aug, k_all, k_aug, v_all, *, past, tq, tk):
    b, t, _ = q.shape
    l = k_all.shape[1]
    n_pairs = N_HEADS // 2
    return pl.pallas_call(
        functools.partial(_attn_kernel, tq=tq, tk=tk, past=past, n_k=l // tk),
        grid=(b, n_pairs, t // tq),
        in_specs=[
            pl.BlockSpec((1, tq, LANES), lambda bi, hp, i: (bi, i, hp)),
            pl.BlockSpec((1, 1, tq, LANES), lambda bi, hp, i: (bi, hp, i, 0)),
            pl.BlockSpec((1, l, LANES), lambda bi, hp, i: (bi, 0, hp)),
            pl.BlockSpec((1, 1, l, LANES), lambda bi, hp, i: (bi, hp, 0, 0)),
            pl.BlockSpec((1, l, LANES), lambda bi, hp, i: (bi, 0, hp)),
        ],
        out_specs=pl.BlockSpec((1, tq, LANES), lambda bi, hp, i: (bi, i, hp)),
        out_shape=jax.ShapeDtypeStruct((b, t, D_ATTN), BF16),
        scratch_shapes=[
            pltpu.VMEM((2, tq, 2 * LANES), BF16),
            pltpu.VMEM((2, 2, tq, tk), F32),
            pltpu.VMEM((2, 2, tq, tk), BF16),
            pltpu.VMEM((2, 2, tq, LANES), F32),
            pltpu.VMEM((2, tq, LANES), F32),
            pltpu.VMEM((2, tq, LANES), F32),
        ],
        compiler_params=_cparams(("arbitrary", "arbitrary", "arbitrary")),
        name="fox_attention",
    )(q, q_aug, k_all, k_aug, v_all)


def _bias_columns(first, second):
    b, h, l, _ = first.shape
    cols = jnp.concatenate([first, second, jnp.zeros((b, h, l, AUG_COLS - 6), BF16)], axis=-1)
    cols = jnp.transpose(cols.reshape(b, h // 2, 2, l, AUG_COLS), (0, 1, 3, 2, 4))
    cols = cols.reshape(b, h // 2, l, 2 * AUG_COLS)
    return jnp.pad(cols, ((0, 0), (0, 0), (0, 0), (0, LANES - 2 * AUG_COLS)))


CONV_ROWS = 32


def _mixout_kernel(x_ref, attn_ref, u_ref, past_ref, wdw_ref, bdw_ref, lng_ref, lnb_ref,
                   woa_ref, woc_ref, x1_ref, ubuf_ref, conv_ref, *, tt):
    ti = pl.program_id(1)

    @pl.when(ti == 0)
    def _():
        ubuf_ref[0:CONV_HALO, :] = past_ref[0]

    @pl.when(ti > 0)
    def _():
        ubuf_ref[0:CONV_HALO, :] = ubuf_ref[tt:tt + CONV_HALO, :]

    ubuf_ref[CONV_HALO:CONV_HALO + tt, :] = u_ref[0]

    lead = CONV_HALO - CONV_STATE
    for c0 in range(0, tt, CONV_ROWS):
        acc = jnp.zeros((CONV_ROWS, D_CONV), F32)
        for w in range(CONV_WIDTH):
            acc = acc + ubuf_ref[c0 + lead + w:c0 + lead + w + CONV_ROWS, :] * wdw_ref[w:w + 1, :]
        y = acc + bdw_ref[...]
        mu = jnp.mean(y, axis=-1, keepdims=True)
        yc = y - mu
        var = jnp.mean(yc * yc, axis=-1, keepdims=True)
        yn = yc * lax.rsqrt(var + LN_EPS) * lng_ref[...] + lnb_ref[...]
        conv_ref[c0:c0 + CONV_ROWS, :] = (yn * jax.nn.sigmoid(yn)).astype(BF16)

    mixed = _dot(attn_ref[0], woa_ref[...]) + _dot(conv_ref[...], woc_ref[...])
    x1_ref[0] = x_ref[0] + mixed


def _mixout(x, attn, u, past_pad, w_dw, b_dw, ln_g, ln_b, wo_a, wo_c):
    b, t, _ = x.shape
    tt = min(512, t)
    tile = lambda bi, ti: (bi, ti, 0)
    fixed = lambda bi, ti: (0, 0)
    return pl.pallas_call(
        functools.partial(_mixout_kernel, tt=tt),
        grid=(b, t // tt),
        in_specs=[
            pl.BlockSpec((1, tt, D_MODEL), tile),
            pl.BlockSpec((1, tt, D_ATTN), tile),
            pl.BlockSpec((1, tt, D_CONV), tile),
            pl.BlockSpec((1, CONV_HALO, D_CONV), lambda bi, ti: (bi, 0, 0)),
            pl.BlockSpec((CONV_HALO, D_CONV), fixed),
            pl.BlockSpec((1, D_CONV), fixed),
            pl.BlockSpec((1, D_CONV), fixed),
            pl.BlockSpec((1, D_CONV), fixed),
            pl.BlockSpec((D_ATTN, D_MODEL), fixed),
            pl.BlockSpec((D_CONV, D_MODEL), fixed),
        ],
        out_specs=pl.BlockSpec((1, tt, D_MODEL), tile),
        out_shape=jax.ShapeDtypeStruct((b, t, D_MODEL), F32),
        scratch_shapes=[
            pltpu.VMEM((tt + CONV_HALO, D_CONV), F32),
            pltpu.VMEM((tt, D_CONV), BF16),
        ],
        compiler_params=_cparams(("arbitrary", "arbitrary")),
        name="mixout",
    )(x, attn, u, past_pad, w_dw, b_dw, ln_g, ln_b, wo_a, wo_c)


PAIR_COLS = [PEER_TOPK // (a + 1) for a in range(PEER_TOPK)]


def _pair_rows(a):
    if a == 0:
        return 0, PEER_TOPK
    return PEER_TOPK + (a - 1) * SUBLANES, SUBLANES


N_PAD_ROWS = sum(_pair_rows(a)[1] - PAIR_COLS[a] for a in range(PEER_TOPK))


def _top_rows(s, k, scatter, exact, n_removed_before=0):
    rows, tt = s.shape
    orig = s
    iota = lax.broadcasted_iota(jnp.int32, s.shape, 0).astype(F32) if exact else None
    vals = []
    rank = jnp.full(s.shape, float(k), F32) if scatter else None
    for a in range(k):
        m = jnp.max(s, axis=0, keepdims=True)
        if exact:
            ix = jnp.min(jnp.where(s == m, iota, float(rows)), axis=0, keepdims=True)
            onehot = iota == ix
        else:
            onehot = s == m
        vals.append(m)
        if scatter:
            rank = jnp.where(onehot, float(a), rank)
        s = jnp.where(onehot, -jnp.inf, s)
    gone = s == -jnp.inf
    removed = jnp.sum(jnp.where(gone, 1.0, 0.0), axis=0, keepdims=True)
    tied = jnp.where(removed != float(k + n_removed_before), 1.0, 0.0)
    e = jnp.where(gone, jnp.exp(orig - vals[0]), 0.0) if scatter else None
    return vals, s, rank, e, tied


def _route_tile(ht, wpqt_ref, keys_ref, cnt0_ref, e0_ref, rank1_ref, e1_ref, *, tt, exact):
    row8 = lax.broadcasted_iota(jnp.int32, (SUBLANES, tt), 0)
    any_tied = jnp.zeros((1, tt), F32)
    for hd in range(PEER_HEADS):
        tops, ranks, es = [], [], []
        for p in range(2):
            r0 = (hd * 2 + p) * PEER_HALF
            qt = _dot(wpqt_ref[r0:r0 + PEER_HALF, :], ht).astype(BF16)
            st = _dot(keys_ref[p, hd], qt)
            vals, _, rank, e, tied = _top_rows(st, PEER_TOPK, True, exact)
            tops.append(vals)
            ranks.append(rank)
            es.append(e)
            any_tied = jnp.maximum(any_tied, tied)
        s1 = jnp.concatenate(tops[1], axis=0)
        blocks = [tops[0][0] + s1]
        for a in range(1, PEER_TOPK):
            blk = tops[0][a] + s1[:SUBLANES]
            if PAIR_COLS[a] < SUBLANES:
                blk = jnp.where(row8 < PAIR_COLS[a], blk, -jnp.inf)
            blocks.append(blk)
        cand = jnp.concatenate(blocks, axis=0)
        best, left, _, _, tied = _top_rows(cand, PEER_TOPK, False, exact, N_PAD_ROWS)
        any_tied = jnp.maximum(any_tied, tied)
        z = jnp.zeros_like(best[0])
        for b in range(PEER_TOPK):
            z = z + jnp.exp(best[b] - best[0])
        cnt0 = jnp.zeros((PEER_NKEYS, tt), F32)
        for a in range(PEER_TOPK):
            start, size = _pair_rows(a)
            taken = left[start:start + size, :] == -jnp.inf
            n_pad = float(size - PAIR_COLS[a])
            cnt_a = jnp.sum(jnp.where(taken, 1.0, 0.0), axis=0, keepdims=True) - n_pad
            cnt0 = jnp.where(ranks[0] == float(a), cnt_a, cnt0)
        cnt0_ref[hd] = cnt0
        e0_ref[hd] = es[0]
        rank1_ref[hd] = ranks[1].astype(rank1_ref.dtype)
        e1_ref[hd] = (es[1] * (1.0 / z)).astype(e1_ref.dtype)
    return any_tied


def _route_kernel(x1_ref, g_ref, wpqt_ref, keys_ref, ht_ref, cnt0_ref, e0_ref, rank1_ref, e1_ref,
                  *, tt):
    x = x1_ref[...]
    ms = jnp.mean(x * x, axis=-1, keepdims=True)
    h = x * lax.rsqrt(ms + RMS_EPS) * g_ref[...]
    ht_ref[...] = h.T.astype(BF16)
    tile = functools.partial(_route_tile, wpqt_ref=wpqt_ref, keys_ref=keys_ref, cnt0_ref=cnt0_ref,
                             e0_ref=e0_ref, rank1_ref=rank1_ref, e1_ref=e1_ref, tt=tt)
    tied = tile(ht_ref[...], exact=False)

    @pl.when(jnp.max(tied) > 0.0)
    def _():
        tile(ht_ref[...], exact=True)


def _route(x1_2d, g_ffn, w_pqt, keys):
    n = x1_2d.shape[0]
    tt = min(256, n)
    fac_shape = (PEER_HEADS, PEER_NKEYS, n)
    fac_spec = pl.BlockSpec((PEER_HEADS, PEER_NKEYS, tt), lambda i: (0, 0, i))
    return pl.pallas_call(
        functools.partial(_route_kernel, tt=tt),
        grid=(n // tt,),
        in_specs=[
            pl.BlockSpec((tt, D_MODEL), lambda i: (i, 0)),
            pl.BlockSpec((1, D_MODEL), lambda i: (0, 0)),
            pl.BlockSpec((2 * PEER_HEADS * PEER_HALF, D_MODEL), lambda i: (0, 0)),
            pl.BlockSpec((2, PEER_HEADS, PEER_NKEYS, PEER_HALF), lambda i: (0, 0, 0, 0)),
        ],
        out_specs=[pl.BlockSpec((D_MODEL, tt), lambda i: (0, i)), fac_spec, fac_spec, fac_spec, fac_spec],
        out_shape=[jax.ShapeDtypeStruct((D_MODEL, n), BF16),
                   jax.ShapeDtypeStruct(fac_shape, F32), jax.ShapeDtypeStruct(fac_shape, F32),
                   jax.ShapeDtypeStruct(fac_shape, BF16), jax.ShapeDtypeStruct(fac_shape, BF16)],
        compiler_params=_cparams(("arbitrary",)),
        name="peer_route",
    )(x1_2d, g_ffn, w_pqt, keys)


EXPERT_CHUNK = 1024
EXPERT_SUB = 256
PACK_ROWS = 16


def _expert_kernel(ht_ref, cnt0_ref, e0_ref, rank1_ref, e1_ref, u_ref, vt_ref, x1_ref, gfin_ref,
                   y_ref, acc_ref, act_ref, *, tt, te):
    c = pl.program_id(1)

    @pl.when(c == 0)
    def _():
        acc_ref[...] = jnp.zeros(acc_ref.shape, F32)

    keys_per_sub = EXPERT_SUB // PEER_NKEYS
    for sc in range(te // EXPERT_SUB):
        hT = _dot(u_ref[sc * EXPERT_SUB:(sc + 1) * EXPERT_SUB, :], ht_ref[...])
        for ii in range(keys_per_sub):
            i = c * (te // PEER_NKEYS) + sc * keys_per_sub + ii
            gate = jnp.zeros((PEER_NKEYS // PACK_ROWS, PACK_ROWS, tt), BF16)
            for hd in range(PEER_HEADS):
                cnt = jnp.broadcast_to(cnt0_ref[hd, pl.ds(i, 1), :], (PACK_ROWS, tt)).astype(BF16)
                e0 = jnp.broadcast_to(e0_ref[hd, pl.ds(i, 1), :], (PACK_ROWS, tt)).astype(BF16)
                rank1 = rank1_ref[hd].reshape(gate.shape)
                e1 = e1_ref[hd].reshape(gate.shape)
                picked = jnp.where(rank1 < cnt[None], e1, jnp.zeros((), BF16))
                gate = gate + picked * e0[None]
            hs = hT[ii * PEER_NKEYS:(ii + 1) * PEER_NKEYS, :]
            gelu = 0.5 * hs * (1.0 + lax.erf(hs * (2.0 ** -0.5)))
            r0 = sc * EXPERT_SUB + ii * PEER_NKEYS
            act_ref[r0:r0 + PEER_NKEYS, :] = gelu.astype(BF16) * gate.reshape(PEER_NKEYS, tt)
    acc_ref[...] += _dot(vt_ref[0], act_ref[...])

    @pl.when(c == pl.num_programs(1) - 1)
    def _():
        x2 = x1_ref[...] + acc_ref[...].T
        ms = jnp.mean(x2 * x2, axis=-1, keepdims=True)
        y_ref[...] = x2 * lax.rsqrt(ms + RMS_EPS) * gfin_ref[...]


def _expert(ht, cnt0, e0, rank1, e1, u_emb, v_embt, x1_2d, g_final):
    n = x1_2d.shape[0]
    tt = min(512, n)
    te = EXPERT_CHUNK
    fac_spec = pl.BlockSpec((PEER_HEADS, PEER_NKEYS, tt), lambda i, c: (0, 0, i))
    return pl.pallas_call(
        functools.partial(_expert_kernel, tt=tt, te=te),
        grid=(n // tt, PEER_EXPERTS // te),
        in_specs=[
            pl.BlockSpec((D_MODEL, tt), lambda i, c: (0, i)),
            fac_spec, fac_spec, fac_spec, fac_spec,
            pl.BlockSpec((te, D_MODEL), lambda i, c: (c, 0)),
            pl.BlockSpec((1, D_MODEL, te), lambda i, c: (c, 0, 0)),
            pl.BlockSpec((tt, D_MODEL), lambda i, c: (i, 0)),
            pl.BlockSpec((1, D_MODEL), lambda i, c: (0, 0)),
        ],
        out_specs=pl.BlockSpec((tt, D_MODEL), lambda i, c: (i, 0)),
        out_shape=jax.ShapeDtypeStruct((n, D_MODEL), F32),
        scratch_shapes=[
            pltpu.VMEM((D_MODEL, tt), F32),
            pltpu.VMEM((te, tt), BF16),
        ],
        compiler_params=_cparams(("arbitrary", "arbitrary")),
        name="peer_expert",
    )(ht, cnt0, e0, rank1, e1, u_emb, v_embt, x1_2d, g_final)


def _pad_time(a, mult):
    pad = (-a.shape[2]) % mult
    return jnp.pad(a, ((0, 0), (0, 0), (0, pad))) if pad else a


def _layer(x, k_past, v_past, logf_past, conv_past, w):
    b, t, _ = x.shape
    n = b * t
    past = 0 if k_past is None else k_past.shape[1]

    q, k, v, kb, vb, logf, u = _inproj(x.reshape(n, D_MODEL), w["g_mix"], w["w_qkv"], w["w_f"],
                                       w["w_ag"], w["b_f"])
    logf = logf.reshape(b, t, N_HEADS)
    kb = kb.reshape(b, t, D_ATTN)
    vb = vb.reshape(b, t, D_ATTN)
    if past:
        lf_all = jnp.concatenate([logf_past, logf], axis=1)
        k_all = jnp.concatenate([k_past.reshape(b, past, D_ATTN).astype(BF16), kb], axis=1)
        v_all = jnp.concatenate([v_past.reshape(b, past, D_ATTN).astype(BF16), vb], axis=1)
    else:
        lf_all, k_all, v_all = logf, kb, vb
    l = past + t
    pieces = _cumsum(_pad_time(jnp.transpose(lf_all, (0, 2, 1)), LANES))
    pieces = jnp.stack([p[:, :, :l] for p in pieces], axis=-1)
    ones = jnp.ones_like(pieces)
    k_aug = _bias_columns(ones, -pieces)
    q_aug = _bias_columns(pieces[:, :, past:], ones[:, :, past:])
    tq = min(512, t)
    tk = tq if past == 0 else l
    attn = _attention(q.reshape(b, t, D_ATTN), q_aug, k_all, k_aug, v_all, past=past, tq=tq, tk=tk)

    if conv_past is None:
        past_pad = jnp.zeros((b, CONV_HALO, D_CONV), F32)
    else:
        past_pad = jnp.pad(conv_past, ((0, 0), (CONV_HALO - CONV_STATE, 0), (0, 0)))
    u3 = u.reshape(b, t, D_CONV)
    x1 = _mixout(x, attn, u3, past_pad, w["w_dw"], w["b_dw"], w["ln_g"], w["ln_b"],
                 w["wo_a"], w["wo_c"])
    u_full_tail = u3 if conv_past is None else jnp.concatenate([conv_past, u3], axis=1)
    conv_state = u_full_tail[:, -CONV_STATE:]

    x1_2d = x1.reshape(n, D_MODEL)
    ht, cnt0, e0, rank1, e1 = _route(x1_2d, w["g_ffn"], w["w_pqt"], w["keys"])
    y = _expert(ht, cnt0, e0, rank1, e1, w["u_emb"], w["v_embt"], x1_2d, w["g_final"])
    return (y.reshape(b, t, D_MODEL), k.reshape(b, t, N_HEADS, HEAD_DIM),
            v.reshape(b, t, N_HEADS, HEAD_DIM), logf, conv_state)


def _prep_weights(g_mix, w_in, b_f, w_dw, b_dw, ln_g, ln_b, w_out, g_ffn, w_pq, sub_keys,
                  u_emb, v_emb, g_final):
    o3 = 3 * D_ATTN
    o4 = o3 + N_HEADS
    row = lambda a: a.reshape(1, -1)
    return {
        "g_mix": row(g_mix),
        "w_qkv": w_in[:, :o3].astype(BF16),
        "w_f": jnp.pad(w_in[:, o3:o4], ((0, 0), (0, LANES - N_HEADS))).astype(BF16),
        "w_ag": w_in[:, o4:].astype(BF16),
        "b_f": jnp.pad(b_f, (0, LANES - N_HEADS)).reshape(1, LANES),
        "w_dw": jnp.pad(w_dw, ((0, CONV_HALO - CONV_WIDTH), (0, 0))),
        "b_dw": row(b_dw), "ln_g": row(ln_g), "ln_b": row(ln_b),
        "wo_a": w_out[:D_ATTN].astype(BF16),
        "wo_c": w_out[D_ATTN:].astype(BF16),
        "g_ffn": row(g_ffn),
        "w_pqt": w_pq.T.astype(BF16),
        "keys": sub_keys.astype(BF16),
        "u_emb": u_emb.astype(BF16),
        "v_embt": jnp.transpose(v_emb.astype(BF16).reshape(-1, EXPERT_CHUNK, D_MODEL), (0, 2, 1)),
        "g_final": row(g_final),
    }


def kernel(x_prompt, x_sample, cache_k, cache_v, cache_logf, state_conv, g_mix, w_in, b_f, w_dw, b_dw,
           ln_g, ln_b, w_out, g_ffn, w_pq, sub_keys, u_emb, v_emb, g_final):
    assert g_mix.shape[0] == 1, "single layer"
    w = _prep_weights(g_mix[0], w_in[0], b_f[0], w_dw[0], b_dw[0], ln_g[0], ln_b[0], w_out[0],
                      g_ffn[0], w_pq[0], sub_keys[0], u_emb[0], v_emb[0], g_final)
    yp, kp, vp, fp, cp = _layer(x_prompt, None, None, None, None, w)
    ys, ks, vs, fs, cs = _layer(x_sample, cache_k[0], cache_v[0], cache_logf[0], state_conv[0], w)
    stack = lambda a: a[None]
    return (yp, ys, stack(kp), stack(vp), stack(fp), stack(cp),
            stack(ks), stack(vs), stack(fs), stack(cs))
```

```python
import functools

import jax
import jax.numpy as jnp
from jax import lax
from jax.experimental import pallas as pl
from jax.experimental.pallas import tpu as pltpu

D_MODEL = 1024
D_ATTN = 512
D_CONV = 512
HEAD_DIM = 64
N_HEADS = 8
CONV_WIDTH = 31
CONV_STATE = CONV_WIDTH - 1
PEER_HEADS = 8
PEER_NKEYS = 128
PEER_EXPERTS = PEER_NKEYS * PEER_NKEYS
PEER_HALF = 128
PEER_TOPK = 16
RMS_EPS = 1e-6
LN_EPS = 1e-5

LANES = 128
SUBLANES = 8
CONV_HALO = 32
NEG_BIG = -1e30
AUG_COLS = 8
VMEM_LIMIT = 56 * 1024 * 1024

F32 = jnp.float32
BF16 = jnp.bfloat16


def _cparams(sem):
    return pltpu.CompilerParams(dimension_semantics=sem, vmem_limit_bytes=VMEM_LIMIT)


def _dot(a, b):
    return jnp.dot(a, b, preferred_element_type=F32)


def _dot_nt(a, b):
    return lax.dot_general(a, b, (((1,), (1,)), ((), ())), preferred_element_type=F32)


def _inproj_kernel(x_ref, g_ref, wqkv_ref, wf_ref, wag_ref, bf_ref,
                   q_ref, k_ref, v_ref, kb_ref, vb_ref, logf_ref, u_ref):
    x = x_ref[...]
    ms = jnp.mean(x * x, axis=-1, keepdims=True)
    h = (x * lax.rsqrt(ms + RMS_EPS) * g_ref[...]).astype(BF16)
    qkv = _dot(h, wqkv_ref[...])
    q_ref[...] = (qkv[:, :D_ATTN] * (HEAD_DIM ** -0.5)).astype(BF16)
    k = qkv[:, D_ATTN:2 * D_ATTN]
    v = qkv[:, 2 * D_ATTN:]
    k_ref[...] = k
    v_ref[...] = v
    kb_ref[...] = k.astype(BF16)
    vb_ref[...] = v.astype(BF16)
    f = _dot(h, wf_ref[...]) + bf_ref[...]
    logf_ref[...] = jax.nn.log_sigmoid(f)[:, :N_HEADS]
    ag = _dot(h, wag_ref[...])
    u_ref[...] = ag[:, :D_CONV] * jax.nn.sigmoid(ag[:, D_CONV:])


def _inproj(x2d, g_mix, w_qkv, w_f, w_ag, b_f):
    n = x2d.shape[0]
    tm = min(512, n)
    row = lambda i: (i, 0)
    fixed = lambda i: (0, 0)
    return pl.pallas_call(
        _inproj_kernel,
        grid=(n // tm,),
        in_specs=[
            pl.BlockSpec((tm, D_MODEL), row),
            pl.BlockSpec((1, D_MODEL), fixed),
            pl.BlockSpec((D_MODEL, 3 * D_ATTN), fixed),
            pl.BlockSpec((D_MODEL, LANES), fixed),
            pl.BlockSpec((D_MODEL, 2 * D_CONV), fixed),
            pl.BlockSpec((1, LANES), fixed),
        ],
        out_specs=[
            pl.BlockSpec((tm, D_ATTN), row),
            pl.BlockSpec((tm, D_ATTN), row),
            pl.BlockSpec((tm, D_ATTN), row),
            pl.BlockSpec((tm, D_ATTN), row),
            pl.BlockSpec((tm, D_ATTN), row),
            pl.BlockSpec((tm, N_HEADS), row),
            pl.BlockSpec((tm, D_CONV), row),
        ],
        out_shape=[
            jax.ShapeDtypeStruct((n, D_ATTN), BF16),
            jax.ShapeDtypeStruct((n, D_ATTN), F32),
            jax.ShapeDtypeStruct((n, D_ATTN), F32),
            jax.ShapeDtypeStruct((n, D_ATTN), BF16),
            jax.ShapeDtypeStruct((n, D_ATTN), BF16),
            jax.ShapeDtypeStruct((n, N_HEADS), F32),
            jax.ShapeDtypeStruct((n, D_CONV), F32),
        ],
        compiler_params=_cparams(("arbitrary",)),
        name="inproj",
    )(x2d, g_mix, w_qkv, w_f, w_ag, b_f)


def _split3(x):
    x1 = x.astype(BF16)
    r1 = x - x1.astype(F32)
    x2 = r1.astype(BF16)
    x3 = (r1 - x2.astype(F32)).astype(BF16)
    return x1, x2, x3


def _cumsum_kernel(lf_ref, tri_ref, hi_ref, mid_ref, lo_ref, *, n_chunks):
    tri = tri_ref[...]

    def body(ci, carry):
        start = pl.multiple_of(ci * LANES, LANES)
        x1, x2, x3 = _split3(lf_ref[0, :, pl.ds(start, LANES)])
        c = _dot(x1, tri) + _dot(x2, tri) + _dot(x3, tri) + carry
        c1, c2, c3 = _split3(c)
        hi_ref[0, :, pl.ds(start, LANES)] = c1
        mid_ref[0, :, pl.ds(start, LANES)] = c2
        lo_ref[0, :, pl.ds(start, LANES)] = c3
        return c[:, LANES - 1:LANES]

    lax.fori_loop(0, n_chunks, body, jnp.zeros((N_HEADS, 1), F32))


def _cumsum(lf):
    b, h, l = lf.shape
    tri = (lax.broadcasted_iota(jnp.int32, (LANES, LANES), 0)
           <= lax.broadcasted_iota(jnp.int32, (LANES, LANES), 1)).astype(BF16)
    spec = pl.BlockSpec((1, h, l), lambda i: (i, 0, 0))
    piece = jax.ShapeDtypeStruct((b, h, l), BF16)
    return pl.pallas_call(
        functools.partial(_cumsum_kernel, n_chunks=l // LANES),
        grid=(b,),
        in_specs=[spec, pl.BlockSpec((LANES, LANES), lambda i: (0, 0))],
        out_specs=[spec, spec, spec],
        out_shape=[piece, piece, piece],
        compiler_params=_cparams(("arbitrary",)),
        name="logf_cumsum",
    )(lf, tri)


def _attn_kernel(q_ref, qa_ref, k_ref, ka_ref, v_ref, o_ref,
                 qcat_ref, s_ref, p_ref, alpha_ref, m_ref, acc_ref, *, tq, tk, past, n_k):
    i = pl.program_id(2)
    lane = lax.broadcasted_iota(jnp.int32, (tq, LANES), 1)
    q = q_ref[0]
    qa = qa_ref[0, 0]
    for hh in range(2):
        in_head = (lane >= hh * HEAD_DIM) & (lane < (hh + 1) * HEAD_DIM)
        in_aug = (lane >= hh * AUG_COLS) & (lane < (hh + 1) * AUG_COLS)
        qcat_ref[hh, :, :LANES] = jnp.where(in_head, q, jnp.zeros_like(q))
        qcat_ref[hh, :, LANES:] = jnp.where(in_aug, qa, jnp.zeros_like(qa))
    m_ref[...] = jnp.full(m_ref.shape, NEG_BIG, F32)
    acc_ref[...] = jnp.zeros(acc_ref.shape, F32)
    p_ref[1] = jnp.zeros(p_ref.shape[1:], p_ref.dtype)
    alpha_ref[1] = jnp.ones(alpha_ref.shape[1:], F32)

    first_q = past + i * tq
    n_full = jnp.minimum((first_q + 1) // tk, n_k)
    n_all = jnp.minimum((first_q + tq - 1) // tk + 1, n_k)
    n_pairs = n_full // 2
    v_lane = lax.broadcasted_iota(jnp.int32, (tk, LANES), 1)

    def scores(j, slot):
        start = pl.multiple_of(j * tk, tk)
        kcat = jnp.concatenate([k_ref[0, pl.ds(start, tk), :], ka_ref[0, 0, pl.ds(start, tk), :]],
                               axis=1)
        for hh in range(2):
            s_ref[slot, hh] = _dot_nt(qcat_ref[hh], kcat)

    def softmax(slot, visible=None):
        for hh in range(2):
            s = s_ref[slot, hh]
            if visible is not None:
                s = jnp.where(visible, s, NEG_BIG)
            m_prev = m_ref[hh]
            m_new = jnp.maximum(m_prev, jnp.max(s, axis=-1, keepdims=True))
            alpha_ref[slot, hh] = jnp.exp(m_prev - m_new)
            p_ref[slot, hh] = jnp.exp(s - m_new[:, :1]).astype(p_ref.dtype)
            m_ref[hh] = m_new

    def accumulate(j, slot):
        start = pl.multiple_of(j * tk, tk)
        v = v_ref[0, pl.ds(start, tk), :]
        for hh in range(2):
            in_head = (v_lane >= hh * HEAD_DIM) & (v_lane < (hh + 1) * HEAD_DIM)
            vh = jnp.where(in_head, v, jnp.ones_like(v))
            acc_ref[hh] = alpha_ref[slot, hh] * acc_ref[hh] + _dot(p_ref[slot, hh], vh)

    def masked_body(j, carry):
        q_pos = first_q + lax.broadcasted_iota(jnp.int32, (tq, tk), 0)
        k_pos = j * tk + lax.broadcasted_iota(jnp.int32, (tq, tk), 1)
        scores(j, 0)
        softmax(0, k_pos <= q_pos)
        accumulate(j, 0)
        return carry

    def single_body(_, carry):
        scores(n_full - 1, 0)
        softmax(0)
        accumulate(n_full - 1, 0)
        return carry

    def pair_body(u, carry):
        last = 2 * n_pairs - 1
        scores(2 * u + 1, 1)
        softmax(0)
        accumulate(jnp.maximum(2 * u - 1, 0), 1)
        scores(jnp.minimum(2 * u + 2, last), 0)
        softmax(1)
        accumulate(2 * u, 0)
        return carry

    lax.fori_loop(n_full, n_all, masked_body, 0)
    lax.fori_loop(0, n_full % 2, single_body, 0)

    @pl.when(n_pairs > 0)
    def _():
        scores(0, 0)

    lax.fori_loop(0, n_pairs, pair_body, 0)

    @pl.when(n_pairs > 0)
    def _():
        accumulate(2 * n_pairs - 1, 1)

    outs = []
    for hh in range(2):
        acc = acc_ref[hh]
        outs.append(acc / pltpu.roll(acc, HEAD_DIM, 1))
    o_ref[0] = jnp.where(lane < HEAD_DIM, outs[0], outs[1]).astype(o_ref.dtype)


def _attention(q, q_aug, k_all, k_aug, v_all, *, past, tq, tk):
    b, t, _ = q.shape
    l = k_all.shape[1]
    n_pairs = N_HEADS // 2
    return pl.pallas_call(
        functools.partial(_attn_kernel, tq=tq, tk=tk, past=past, n_k=l // tk),
        grid=(b, n_pairs, t // tq),
        in_specs=[
            pl.BlockSpec((1, tq, LANES), lambda bi, hp, i: (bi, i, hp)),
            pl.BlockSpec((1, 1, tq, LANES), lambda bi, hp, i: (bi, hp, i, 0)),
            pl.BlockSpec((1, l, LANES), lambda bi, hp, i: (bi, 0, hp)),
            pl.BlockSpec((1, 1, l, LANES), lambda bi, hp, i: (bi, hp, 0, 0)),
            pl.BlockSpec((1, l, LANES), lambda bi, hp, i: (bi, 0, hp)),
        ],
        out_specs=pl.BlockSpec((1, tq, LANES), lambda bi, hp, i: (bi, i, hp)),
        out_shape=jax.ShapeDtypeStruct((b, t, D_ATTN), BF16),
        scratch_shapes=[
            pltpu.VMEM((2, tq, 2 * LANES), BF16),
            pltpu.VMEM((2, 2, tq, tk), F32),
            pltpu.VMEM((2, 2, tq, tk), BF16),
            pltpu.VMEM((2, 2, tq, LANES), F32),
            pltpu.VMEM((2, tq, LANES), F32),
            pltpu.VMEM((2, tq, LANES), F32),
        ],
        compiler_params=_cparams(("arbitrary", "arbitrary", "arbitrary")),
        name="fox_attention",
    )(q, q_aug, k_all, k_aug, v_all)


def _bias_columns(first, second):
    b, h, l, _ = first.shape
    cols = jnp.concatenate([first, second, jnp.zeros((b, h, l, AUG_COLS - 6), BF16)], axis=-1)
    cols = jnp.transpose(cols.reshape(b, h // 2, 2, l, AUG_COLS), (0, 1, 3, 2, 4))
    cols = cols.reshape(b, h // 2, l, 2 * AUG_COLS)
    return jnp.pad(cols, ((0, 0), (0, 0), (0, 0), (0, LANES - 2 * AUG_COLS)))


CONV_ROWS = 32


def _mixout_kernel(x_ref, attn_ref, u_ref, past_ref, wdw_ref, bdw_ref, lng_ref, lnb_ref,
                   woa_ref, woc_ref, x1_ref, ubuf_ref, conv_ref, *, tt):
    ti = pl.program_id(1)

    @pl.when(ti == 0)
    def _():
        ubuf_ref[0:CONV_HALO, :] = past_ref[0]

    @pl.when(ti > 0)
    def _():
        ubuf_ref[0:CONV_HALO, :] = ubuf_ref[tt:tt + CONV_HALO, :]

    ubuf_ref[CONV_HALO:CONV_HALO + tt, :] = u_ref[0]

    lead = CONV_HALO - CONV_STATE
    for c0 in range(0, tt, CONV_ROWS):
        acc = jnp.zeros((CONV_ROWS, D_CONV), F32)
        for w in range(CONV_WIDTH):
            acc = acc + ubuf_ref[c0 + lead + w:c0 + lead + w + CONV_ROWS, :] * wdw_ref[w:w + 1, :]
        y = acc + bdw_ref[...]
        mu = jnp.mean(y, axis=-1, keepdims=True)
        yc = y - mu
        var = jnp.mean(yc * yc, axis=-1, keepdims=True)
        yn = yc * lax.rsqrt(var + LN_EPS) * lng_ref[...] + lnb_ref[...]
        conv_ref[c0:c0 + CONV_ROWS, :] = (yn * jax.nn.sigmoid(yn)).astype(BF16)

    mixed = _dot(attn_ref[0], woa_ref[...]) + _dot(conv_ref[...], woc_ref[...])
    x1_ref[0] = x_ref[0] + mixed


def _mixout(x, attn, u, past_pad, w_dw, b_dw, ln_g, ln_b, wo_a, wo_c):
    b, t, _ = x.shape
    tt = min(512, t)
    tile = lambda bi, ti: (bi, ti, 0)
    fixed = lambda bi, ti: (0, 0)
    return pl.pallas_call(
        functools.partial(_mixout_kernel, tt=tt),
        grid=(b, t // tt),
        in_specs=[
            pl.BlockSpec((1, tt, D_MODEL), tile),
            pl.BlockSpec((1, tt, D_ATTN), tile),
            pl.BlockSpec((1, tt, D_CONV), tile),
            pl.BlockSpec((1, CONV_HALO, D_CONV), lambda bi, ti: (bi, 0, 0)),
            pl.BlockSpec((CONV_HALO, D_CONV), fixed),
            pl.BlockSpec((1, D_CONV), fixed),
            pl.BlockSpec((1, D_CONV), fixed),
            pl.BlockSpec((1, D_CONV), fixed),
            pl.BlockSpec((D_ATTN, D_MODEL), fixed),
            pl.BlockSpec((D_CONV, D_MODEL), fixed),
        ],
        out_specs=pl.BlockSpec((1, tt, D_MODEL), tile),
        out_shape=jax.ShapeDtypeStruct((b, t, D_MODEL), F32),
        scratch_shapes=[
            pltpu.VMEM((tt + CONV_HALO, D_CONV), F32),
            pltpu.VMEM((tt, D_CONV), BF16),
        ],
        compiler_params=_cparams(("arbitrary", "arbitrary")),
        name="mixout",
    )(x, attn, u, past_pad, w_dw, b_dw, ln_g, ln_b, wo_a, wo_c)


PAIR_COLS = [PEER_TOPK // (a + 1) for a in range(PEER_TOPK)]


def _pair_rows(a):
    if a == 0:
        return 0, PEER_TOPK
    return PEER_TOPK + (a - 1) * SUBLANES, SUBLANES


N_PAD_ROWS = sum(_pair_rows(a)[1] - PAIR_COLS[a] for a in range(PEER_TOPK))


def _top_rows(s, k, scatter, exact, n_removed_before=0):
    rows, tt = s.shape
    orig = s
    iota = lax.broadcasted_iota(jnp.int32, s.shape, 0).astype(F32) if exact else None
    vals = []
    rank = jnp.full(s.shape, float(k), F32) if scatter else None
    for a in range(k):
        m = jnp.max(s, axis=0, keepdims=True)
        if exact:
            ix = jnp.min(jnp.where(s == m, iota, float(rows)), axis=0, keepdims=True)
            onehot = iota == ix
        else:
            onehot = s == m
        vals.append(m)
        if scatter:
            rank = jnp.where(onehot, float(a), rank)
        s = jnp.where(onehot, -jnp.inf, s)
    gone = s == -jnp.inf
    removed = jnp.sum(jnp.where(gone, 1.0, 0.0), axis=0, keepdims=True)
    tied = jnp.where(removed != float(k + n_removed_before), 1.0, 0.0)
    e = jnp.where(gone, jnp.exp(orig - vals[0]), 0.0) if scatter else None
    return vals, s, rank, e, tied


def _route_head(hd, ht_ref, wpqt_ref, keys_ref, cnt0_ref, e0_ref, rank1_ref, e1_ref, *, tt, exact):
    ht = ht_ref[...]
    row8 = lax.broadcasted_iota(jnp.int32, (SUBLANES, tt), 0)
    any_tied = jnp.zeros((1, tt), F32)
    tops, ranks, es = [], [], []
    for p in range(2):
        r0 = (hd * 2 + p) * PEER_HALF
        qt = _dot(wpqt_ref[r0:r0 + PEER_HALF, :], ht).astype(BF16)
        st = _dot(keys_ref[p, hd], qt)
        vals, _, rank, e, tied = _top_rows(st, PEER_TOPK, True, exact)
        tops.append(vals)
        ranks.append(rank)
        es.append(e)
        any_tied = jnp.maximum(any_tied, tied)
    s1 = jnp.concatenate(tops[1], axis=0)
    blocks = [tops[0][0] + s1]
    for a in range(1, PEER_TOPK):
        blk = tops[0][a] + s1[:SUBLANES]
        if PAIR_COLS[a] < SUBLANES:
            blk = jnp.where(row8 < PAIR_COLS[a], blk, -jnp.inf)
        blocks.append(blk)
    cand = jnp.concatenate(blocks, axis=0)
    best, left, _, _, tied = _top_rows(cand, PEER_TOPK, False, exact, N_PAD_ROWS)
    any_tied = jnp.maximum(any_tied, tied)
    z = jnp.zeros_like(best[0])
    for b in range(PEER_TOPK):
        z = z + jnp.exp(best[b] - best[0])
    cnt0 = jnp.zeros((PEER_NKEYS, tt), F32)
    for a in range(PEER_TOPK):
        start, size = _pair_rows(a)
        taken = left[start:start + size, :] == -jnp.inf
        n_pad = float(size - PAIR_COLS[a])
        cnt_a = jnp.sum(jnp.where(taken, 1.0, 0.0), axis=0, keepdims=True) - n_pad
        cnt0 = jnp.where(ranks[0] == float(a), cnt_a, cnt0)
    cnt0_ref[hd] = cnt0
    e0_ref[hd] = es[0]
    rank1_ref[hd] = ranks[1].astype(rank1_ref.dtype)
    e1_ref[hd] = (es[1] * (1.0 / z)).astype(e1_ref.dtype)
    return any_tied


def _route_kernel(x1_ref, g_ref, wpqt_ref, keys_ref, ht_ref, cnt0_ref, e0_ref, rank1_ref, e1_ref,
                  *, tt):
    x = x1_ref[...]
    ms = jnp.mean(x * x, axis=-1, keepdims=True)
    h = x * lax.rsqrt(ms + RMS_EPS) * g_ref[...]
    ht_ref[...] = h.T.astype(BF16)
    head = functools.partial(_route_head, ht_ref=ht_ref, wpqt_ref=wpqt_ref, keys_ref=keys_ref,
                             cnt0_ref=cnt0_ref, e0_ref=e0_ref, rank1_ref=rank1_ref, e1_ref=e1_ref,
                             tt=tt)
    for hd in range(PEER_HEADS):
        tied = head(hd, exact=False)

        @pl.when(jnp.max(tied) > 0.0)
        def _():
            head(hd, exact=True)


def _route(x1_2d, g_ffn, w_pqt, keys):
    n = x1_2d.shape[0]
    tt = min(256, n)
    fac_shape = (PEER_HEADS, PEER_NKEYS, n)
    fac_spec = pl.BlockSpec((PEER_HEADS, PEER_NKEYS, tt), lambda i: (0, 0, i))
    return pl.pallas_call(
        functools.partial(_route_kernel, tt=tt),
        grid=(n // tt,),
        in_specs=[
            pl.BlockSpec((tt, D_MODEL), lambda i: (i, 0)),
            pl.BlockSpec((1, D_MODEL), lambda i: (0, 0)),
            pl.BlockSpec((2 * PEER_HEADS * PEER_HALF, D_MODEL), lambda i: (0, 0)),
            pl.BlockSpec((2, PEER_HEADS, PEER_NKEYS, PEER_HALF), lambda i: (0, 0, 0, 0)),
        ],
        out_specs=[pl.BlockSpec((D_MODEL, tt), lambda i: (0, i)), fac_spec, fac_spec, fac_spec, fac_spec],
        out_shape=[jax.ShapeDtypeStruct((D_MODEL, n), BF16),
                   jax.ShapeDtypeStruct(fac_shape, F32), jax.ShapeDtypeStruct(fac_shape, F32),
                   jax.ShapeDtypeStruct(fac_shape, BF16), jax.ShapeDtypeStruct(fac_shape, BF16)],
        compiler_params=_cparams(("arbitrary",)),
        name="peer_route",
    )(x1_2d, g_ffn, w_pqt, keys)


EXPERT_CHUNK = 1024
EXPERT_SUB = 256
PACK_ROWS = 16


def _expert_kernel(ht_ref, cnt0_ref, e0_ref, rank1_ref, e1_ref, u_ref, vt_ref, x1_ref, gfin_ref,
                   y_ref, acc_ref, act_ref, *, tt, te):
    c = pl.program_id(1)

    @pl.when(c == 0)
    def _():
        acc_ref[...] = jnp.zeros(acc_ref.shape, F32)

    keys_per_sub = EXPERT_SUB // PEER_NKEYS
    for sc in range(te // EXPERT_SUB):
        hT = _dot(u_ref[sc * EXPERT_SUB:(sc + 1) * EXPERT_SUB, :], ht_ref[...])
        for ii in range(keys_per_sub):
            i = c * (te // PEER_NKEYS) + sc * keys_per_sub + ii
            gate = jnp.zeros((PEER_NKEYS // PACK_ROWS, PACK_ROWS, tt), BF16)
            for hd in range(PEER_HEADS):
                cnt = jnp.broadcast_to(cnt0_ref[hd, pl.ds(i, 1), :], (PACK_ROWS, tt)).astype(BF16)
                e0 = jnp.broadcast_to(e0_ref[hd, pl.ds(i, 1), :], (PACK_ROWS, tt)).astype(BF16)
                rank1 = rank1_ref[hd].reshape(gate.shape)
                e1 = e1_ref[hd].reshape(gate.shape)
                picked = jnp.where(rank1 < cnt[None], e1, jnp.zeros((), BF16))
                gate = gate + picked * e0[None]
            hs = hT[ii * PEER_NKEYS:(ii + 1) * PEER_NKEYS, :]
            gelu = 0.5 * hs * (1.0 + lax.erf(hs * (2.0 ** -0.5)))
            r0 = sc * EXPERT_SUB + ii * PEER_NKEYS
            act_ref[r0:r0 + PEER_NKEYS, :] = gelu.astype(BF16) * gate.reshape(PEER_NKEYS, tt)
    acc_ref[...] += _dot(vt_ref[0], act_ref[...])

    @pl.when(c == pl.num_programs(1) - 1)
    def _():
        x2 = x1_ref[...] + acc_ref[...].T
        ms = jnp.mean(x2 * x2, axis=-1, keepdims=True)
        y_ref[...] = x2 * lax.rsqrt(ms + RMS_EPS) * gfin_ref[...]


def _expert(ht, cnt0, e0, rank1, e1, u_emb, v_embt, x1_2d, g_final):
    n = x1_2d.shape[0]
    tt = min(512, n)
    te = EXPERT_CHUNK
    fac_spec = pl.BlockSpec((PEER_HEADS, PEER_NKEYS, tt), lambda i, c: (0, 0, i))
    return pl.pallas_call(
        functools.partial(_expert_kernel, tt=tt, te=te),
        grid=(n // tt, PEER_EXPERTS // te),
        in_specs=[
            pl.BlockSpec((D_MODEL, tt), lambda i, c: (0, i)),
            fac_spec, fac_spec, fac_spec, fac_spec,
            pl.BlockSpec((te, D_MODEL), lambda i, c: (c, 0)),
            pl.BlockSpec((1, D_MODEL, te), lambda i, c: (c, 0, 0)),
            pl.BlockSpec((tt, D_MODEL), lambda i, c: (i, 0)),
            pl.BlockSpec((1, D_MODEL), lambda i, c: (0, 0)),
        ],
        out_specs=pl.BlockSpec((tt, D_MODEL), lambda i, c: (i, 0)),
        out_shape=jax.ShapeDtypeStruct((n, D_MODEL), F32),
        scratch_shapes=[
            pltpu.VMEM((D_MODEL, tt), F32),
            pltpu.VMEM((te, tt), BF16),
        ],
        compiler_params=_cparams(("arbitrary", "arbitrary")),
        name="peer_expert",
    )(ht, cnt0, e0, rank1, e1, u_emb, v_embt, x1_2d, g_final)


def _pad_time(a, mult):
    pad = (-a.shape[2]) % mult
    return jnp.pad(a, ((0, 0), (0, 0), (0, pad))) if pad else a


def _layer(x, k_past, v_past, logf_past, conv_past, w):
    b, t, _ = x.shape
    n = b * t
    past = 0 if k_past is None else k_past.shape[1]

    q, k, v, kb, vb, logf, u = _inproj(x.reshape(n, D_MODEL), w["g_mix"], w["w_qkv"], w["w_f"],
                                       w["w_ag"], w["b_f"])
    logf = logf.reshape(b, t, N_HEADS)
    kb = kb.reshape(b, t, D_ATTN)
    vb = vb.reshape(b, t, D_ATTN)
    if past:
        lf_all = jnp.concatenate([logf_past, logf], axis=1)
        k_all = jnp.concatenate([k_past.reshape(b, past, D_ATTN).astype(BF16), kb], axis=1)
        v_all = jnp.concatenate([v_past.reshape(b, past, D_ATTN).astype(BF16), vb], axis=1)
    else:
        lf_all, k_all, v_all = logf, kb, vb
    l = past + t
    pieces = _cumsum(_pad_time(jnp.transpose(lf_all, (0, 2, 1)), LANES))
    pieces = jnp.stack([p[:, :, :l] for p in pieces], axis=-1)
    ones = jnp.ones_like(pieces)
    k_aug = _bias_columns(ones, -pieces)
    q_aug = _bias_columns(pieces[:, :, past:], ones[:, :, past:])
    tq = min(512, t)
    tk = tq if past == 0 else l
    attn = _attention(q.reshape(b, t, D_ATTN), q_aug, k_all, k_aug, v_all, past=past, tq=tq, tk=tk)

    if conv_past is None:
        past_pad = jnp.zeros((b, CONV_HALO, D_CONV), F32)
    else:
        past_pad = jnp.pad(conv_past, ((0, 0), (CONV_HALO - CONV_STATE, 0), (0, 0)))
    u3 = u.reshape(b, t, D_CONV)
    x1 = _mixout(x, attn, u3, past_pad, w["w_dw"], w["b_dw"], w["ln_g"], w["ln_b"],
                 w["wo_a"], w["wo_c"])
    u_full_tail = u3 if conv_past is None else jnp.concatenate([conv_past, u3], axis=1)
    conv_state = u_full_tail[:, -CONV_STATE:]

    x1_2d = x1.reshape(n, D_MODEL)
    ht, cnt0, e0, rank1, e1 = _route(x1_2d, w["g_ffn"], w["w_pqt"], w["keys"])
    y = _expert(ht, cnt0, e0, rank1, e1, w["u_emb"], w["v_embt"], x1_2d, w["g_final"])
    return (y.reshape(b, t, D_MODEL), k.reshape(b, t, N_HEADS, HEAD_DIM),
            v.reshape(b, t, N_HEADS, HEAD_DIM), logf, conv_state)


def _prep_weights(g_mix, w_in, b_f, w_dw, b_dw, ln_g, ln_b, w_out, g_ffn, w_pq, sub_keys,
                  u_emb, v_emb, g_final):
    o3 = 3 * D_ATTN
    o4 = o3 + N_HEADS
    row = lambda a: a.reshape(1, -1)
    return {
        "g_mix": row(g_mix),
        "w_qkv": w_in[:, :o3].astype(BF16),
        "w_f": jnp.pad(w_in[:, o3:o4], ((0, 0), (0, LANES - N_HEADS))).astype(BF16),
        "w_ag": w_in[:, o4:].astype(BF16),
        "b_f": jnp.pad(b_f, (0, LANES - N_HEADS)).reshape(1, LANES),
        "w_dw": jnp.pad(w_dw, ((0, CONV_HALO - CONV_WIDTH), (0, 0))),
        "b_dw": row(b_dw), "ln_g": row(ln_g), "ln_b": row(ln_b),
        "wo_a": w_out[:D_ATTN].astype(BF16),
        "wo_c": w_out[D_ATTN:].astype(BF16),
        "g_ffn": row(g_ffn),
        "w_pqt": w_pq.T.astype(BF16),
        "keys": sub_keys.astype(BF16),
        "u_emb": u_emb.astype(BF16),
        "v_embt": jnp.transpose(v_emb.astype(BF16).reshape(-1, EXPERT_CHUNK, D_MODEL), (0, 2, 1)),
        "g_final": row(g_final),
    }


def kernel(x_prompt, x_sample, cache_k, cache_v, cache_logf, state_conv, g_mix, w_in, b_f, w_dw, b_dw,
           ln_g, ln_b, w_out, g_ffn, w_pq, sub_keys, u_emb, v_emb, g_final):
    assert g_mix.shape[0] == 1, "single layer"
    w = _prep_weights(g_mix[0], w_in[0], b_f[0], w_dw[0], b_dw[0], ln_g[0], ln_b[0], w_out[0],
                      g_ffn[0], w_pq[0], sub_keys[0], u_emb[0], v_emb[0], g_final)
    yp, kp, vp, fp, cp = _layer(x_prompt, None, None, None, None, w)
    ys, ks, vs, fs, cs = _layer(x_sample, cache_k[0], cache_v[0], cache_logf[0], state_conv[0], w)
    stack = lambda a: a[None]
    return (yp, ys, stack(kp), stack(vp), stack(fp), stack(cp),
            stack(ks), stack(vs), stack(fs), stack(cs))
```
